```python
import math
import jax, jax.numpy as jnp
from jax import lax
import numpy as np

D_MODEL = 1024
BATCH = 4
SEQ = 4096
DEPTH = 2

EXPAND = 2
D_INNER = EXPAND * D_MODEL
IN_COLS = 3 * D_INNER
POOL_WIDTH = D_INNER // 2
POOL_GROUPS = 4
POOL_GROUP_DIM = POOL_WIDTH // POOL_GROUPS
POOL_WINDOWS = (2, 4, 8, 16)
ATTN_WIDTH = D_INNER - POOL_WIDTH
N_DIFF_HEADS = 8
DIFF_VDIM = ATTN_WIDTH // N_DIFF_HEADS
DIFF_QKDIM = DIFF_VDIM // 2
ROPE_THETA = 500000.0
ROPE_DIM = DIFF_QKDIM // 4
Q_BLOCK = 128
SGU_CHUNK = 128
SGU_GROUPS = 8
SGU_GROUP_DIM = D_INNER // SGU_GROUPS
EPS = 1e-6
N_AB = (DEPTH + 1) // 2
N_C = DEPTH // 2

kernel_name = "hybrid_pool_diffattn_sgu_trunk"


def rms_norm(x, g, eps=EPS):
    xf = x.astype(jnp.float32)
    y = xf * lax.rsqrt(jnp.mean(xf * xf, axis=-1, keepdims=True) + eps)
    return (y * g.astype(jnp.float32)).astype(x.dtype)


def multiscale_pool(a, pool_w, pool_scale):
    B, S, _ = a.shape
    ag = a.reshape(B, S, POOL_GROUPS, POOL_GROUP_DIM).astype(jnp.float32)
    csum = lax.cumsum(ag, axis=1)
    pos = jnp.arange(S)
    means = []
    for g, w in enumerate(POOL_WINDOWS):
        c = csum[:, :, g]
        lagged = jnp.pad(c, ((0, 0), (w, 0), (0, 0)))[:, :S]
        cnt = jnp.minimum(pos + 1, w).astype(jnp.float32)[None, :, None]
        means.append((c - lagged) / cnt)
    pooled = (jnp.stack(means, axis=2) - ag).astype(a.dtype)
    y = jnp.einsum('bsgc,gcd->bsgd', pooled, pool_w)
    y = y * pool_scale.reshape(POOL_GROUPS, POOL_GROUP_DIM)
    return y.reshape(B, S, POOL_WIDTH)


def apply_partial_rope(t, positions):
    half = ROPE_DIM // 2
    inv_freq = jnp.power(ROPE_THETA, -jnp.arange(half, dtype=jnp.float32) * 2.0 / ROPE_DIM)
    ang = positions.astype(jnp.float32)[:, :, None] * inv_freq
    cos = jnp.cos(ang)[:, :, None, None, :]
    sin = jnp.sin(ang)[:, :, None, None, :]
    tf = t.astype(jnp.float32)
    x1 = tf[..., :half]
    x2 = tf[..., half:ROPE_DIM]
    rot = jnp.concatenate([x1 * cos - x2 * sin, x2 * cos + x1 * sin, tf[..., ROPE_DIM:]], axis=-1)
    return rot.astype(t.dtype)


def diff_attention(q, k, v, lam, positions):
    B, S = q.shape[:2]
    q = apply_partial_rope(q, positions)
    k = apply_partial_rope(k, positions)
    nb = S // Q_BLOCK
    qb = q.reshape(B, nb, Q_BLOCK, N_DIFF_HEADS, 2, DIFF_QKDIM).swapaxes(0, 1)
    kpos = jnp.arange(S)
    scale = DIFF_QKDIM ** -0.5

    def one_block(args):
        q_blk, blk = args
        s = jnp.einsum('bqhcd,bkhcd->bhcqk', q_blk, k).astype(jnp.float32) * scale
        qpos = blk * Q_BLOCK + jnp.arange(Q_BLOCK)
        s = jnp.where(kpos[None, :] <= qpos[:, None], s, -jnp.inf)
        p = jax.nn.softmax(s, axis=-1)
        w = (p[:, :, 0] - lam * p[:, :, 1]).astype(v.dtype)
        return jnp.einsum('bhqk,bkhd->bqhd', w, v)

    o = lax.map(one_block, (qb, jnp.arange(nb)))
    return o.swapaxes(0, 1).reshape(B, S, N_DIFF_HEADS, DIFF_VDIM)


def chunked_sgu(u, v, ln_g, ln_b, w_s, b_s):
    B, S, _ = u.shape
    vf = v.astype(jnp.float32)
    mu = jnp.mean(vf, axis=-1, keepdims=True)
    var = jnp.mean(jnp.square(vf - mu), axis=-1, keepdims=True)
    vn = ((vf - mu) * lax.rsqrt(var + EPS) * ln_g + ln_b).astype(v.dtype)
    nc = S // SGU_CHUNK
    vc = vn.reshape(B, nc, SGU_CHUNK, SGU_GROUPS, SGU_GROUP_DIM)
    mask = jnp.tril(jnp.ones((SGU_CHUNK, SGU_CHUNK), dtype=bool))
    ws = jnp.where(mask[None], w_s, jnp.zeros_like(w_s))
    mixed = jnp.einsum('gts,bcsgd->bctgd', ws, vc) + b_s.T[None, None, :, :, None]
    return u * mixed.reshape(B, S, D_INNER)


def setup_inputs(seed: int = 0) -> dict:
    key = jax.random.key(seed)
    ks = jax.random.split(key, 20)
    f32 = jnp.float32
    nrm = lambda k, shape, s: (jax.random.normal(k, shape, f32) * s).astype(f32)
    x = jax.random.normal(ks[0], (BATCH, SEQ, D_MODEL), f32)
    positions = jnp.broadcast_to(jnp.arange(SEQ, dtype=jnp.int32), (BATCH, SEQ))
    pre_norm = 1.0 + nrm(ks[1], (DEPTH, D_MODEL), 0.02)
    post_norm = 1.0 + nrm(ks[2], (DEPTH, D_MODEL), 0.02)
    w_in = nrm(ks[3], (DEPTH, D_MODEL, IN_COLS), D_MODEL ** -0.5)
    w_out = nrm(ks[4], (DEPTH, D_INNER, D_MODEL), D_INNER ** -0.5)
    pool_w = nrm(ks[5], (N_AB, POOL_GROUPS, POOL_GROUP_DIM, POOL_GROUP_DIM), POOL_GROUP_DIM ** -0.5)
    pool_scale = 1.0 + nrm(ks[6], (N_AB, POOL_WIDTH), 0.02)
    lam_q1 = nrm(ks[7], (N_AB, DIFF_QKDIM), 0.1)
    lam_k1 = nrm(ks[8], (N_AB, DIFF_QKDIM), 0.1)
    lam_q2 = nrm(ks[9], (N_AB, DIFF_QKDIM), 0.1)
    lam_k2 = nrm(ks[10], (N_AB, DIFF_QKDIM), 0.1)
    diff_subln = 1.0 + nrm(ks[11], (N_AB, DIFF_VDIM), 0.02)
    sgu_ln_g = 1.0 + nrm(ks[12], (N_C, D_INNER), 0.02)
    sgu_ln_b = nrm(ks[13], (N_C, D_INNER), 0.02)
    sgu_w = nrm(ks[14], (N_C, SGU_GROUPS, SGU_CHUNK, SGU_CHUNK), SGU_CHUNK ** -0.5)
    sgu_b = 1.0 + nrm(ks[15], (N_C, SGU_GROUPS, SGU_CHUNK), 0.02)
    return {"x": x, "positions": positions, "pre_norm": pre_norm, "post_norm": post_norm,
            "w_in": w_in, "w_out": w_out, "pool_w": pool_w, "pool_scale": pool_scale,
            "lam_q1": lam_q1, "lam_k1": lam_k1, "lam_q2": lam_q2, "lam_k2": lam_k2,
            "diff_subln": diff_subln, "sgu_ln_g": sgu_ln_g, "sgu_ln_b": sgu_ln_b,
            "sgu_w": sgu_w, "sgu_b": sgu_b}


def reference(x, positions, pre_norm, post_norm, w_in, w_out, pool_w, pool_scale,
              lam_q1, lam_k1, lam_q2, lam_k2, diff_subln, sgu_ln_g, sgu_ln_b, sgu_w, sgu_b):
    B, S, _ = x.shape
    for layer in range(DEPTH):
        h = rms_norm(x, pre_norm[layer])
        proj = jnp.einsum('bsd,de->bse', h, w_in[layer])
        if layer % 2 == 0:
            i = layer // 2
            a = proj[..., :POOL_WIDTH]
            q = proj[..., POOL_WIDTH:2 * POOL_WIDTH].reshape(B, S, N_DIFF_HEADS, 2, DIFF_QKDIM)
            k = proj[..., 2 * POOL_WIDTH:3 * POOL_WIDTH].reshape(B, S, N_DIFF_HEADS, 2, DIFF_QKDIM)
            v = proj[..., 3 * POOL_WIDTH:D_INNER + ATTN_WIDTH * 2].reshape(B, S, N_DIFF_HEADS, DIFF_VDIM)
            gate = proj[..., 2 * D_INNER:]
            pool_out = multiscale_pool(a, pool_w[i], pool_scale[i])
            lam_init = 0.8 - 0.6 * math.exp(-0.3 * layer)
            lam = (jnp.exp(jnp.sum(lam_q1[i].astype(jnp.float32) * lam_k1[i].astype(jnp.float32)))
                   - jnp.exp(jnp.sum(lam_q2[i].astype(jnp.float32) * lam_k2[i].astype(jnp.float32)))
                   + lam_init)
            o = diff_attention(q, k, v, lam, positions)
            o = rms_norm(o, diff_subln[i], eps=1e-5) * (1.0 - lam_init)
            y = jnp.concatenate([pool_out, o.reshape(B, S, ATTN_WIDTH)], axis=-1)
        else:
            i = layer // 2
            u = jax.nn.gelu(proj[..., :D_INNER])
            v = jax.nn.gelu(proj[..., D_INNER:2 * D_INNER])
            gate = proj[..., 2 * D_INNER:]
            y = chunked_sgu(u, v, sgu_ln_g[i], sgu_ln_b[i], sgu_w[i], sgu_b[i])
        out = jnp.einsum('bse,ed->bsd', jax.nn.silu(gate) * y, w_out[layer])
        x = x + rms_norm(out, post_norm[layer])
    return x
```

```python
import functools
import math

import jax
import jax.numpy as jnp
from jax import lax
from jax.experimental import pallas as pl
from jax.experimental.pallas import tpu as pltpu

D_MODEL = 1024
D_INNER = 2048
IN_COLS = 3 * D_INNER
POOL_WIDTH = 1024
POOL_GROUPS = 4
POOL_GROUP_DIM = 256
POOL_WINDOWS = (2, 4, 8, 16)
POOL_HALO = 16
ATTN_WIDTH = 1024
N_DIFF_HEADS = 8
DIFF_VDIM = 128
DIFF_QKDIM = 64
ROPE_THETA = 500000.0
ROPE_DIM = 16
SGU_CHUNK = 128
SGU_GROUPS = 8
SGU_GROUP_DIM = 256
EPS = 1e-6
SUBLN_EPS = 1e-5
LANES = 128

TM_IN0 = 256
TQ = 256
TK = 512
TM_OUT0 = 512
TM_L1 = 256
VMEM_LIMIT = 56 * 1024 * 1024

_NT = (((1,), (1,)), ((), ()))


def _rms_norm(x, g, eps):
    return x * lax.rsqrt(jnp.mean(x * x, axis=-1, keepdims=True) + eps) * g


def _silu(x):
    return x / (1.0 + jnp.exp(-x))


def _gelu_tanh(x):
    c = math.sqrt(2.0 / math.pi)
    return 0.5 * x * (1.0 + jnp.tanh(c * (x + 0.044715 * (x * x * x))))


def _dot(a, b):
    return jnp.dot(a, b, preferred_element_type=jnp.float32)


def _in0_kernel(x_ref, pos_ref, g_ref, w_ref, pw_ref, ps_ref, invf_ref, m1_ref, m2_ref,
                q_ref, k_ref, v_ref, ya_ref, sgb_ref, aext_ref):
    i = pl.program_id(1)
    tm = x_ref.shape[0]
    h = _rms_norm(x_ref[...], g_ref[...], EPS).astype(jnp.bfloat16)

    def proj(col):
        return _dot(h, w_ref[:, col * 1024:(col + 1) * 1024])

    ang = pos_ref[...] * invf_ref[...]
    cs = jnp.cos(ang)
    sn = jnp.sin(ang)
    s_lo = sn * m1_ref[...]
    s_hi = sn * m2_ref[...]
    half = ROPE_DIM // 2

    def rope_store(t, out_ref, scale):
        for c in range(ATTN_WIDTH // LANES):
            tc = t[:, c * LANES:(c + 1) * LANES]
            up = pltpu.roll(tc, LANES - half, axis=1)
            dn = pltpu.roll(tc, half, axis=1)
            r = tc * cs + up * s_lo + dn * s_hi
            if scale != 1.0:
                r = r * scale
            out_ref[:, c * LANES:(c + 1) * LANES] = r.astype(out_ref.dtype)

    rope_store(proj(1), q_ref, DIFF_QKDIM ** -0.5)
    rope_store(proj(2), k_ref, 1.0)
    v_ref[...] = proj(3).astype(v_ref.dtype)
    sgb_ref[...] = _silu(proj(5)).astype(sgb_ref.dtype)

    @pl.when(i == 0)
    def _():
        aext_ref[0:POOL_HALO, :] = jnp.zeros((POOL_HALO, POOL_WIDTH), jnp.float32)

    @pl.when(i > 0)
    def _():
        aext_ref[0:POOL_HALO, :] = aext_ref[tm:tm + POOL_HALO, :]

    aext_ref[POOL_HALO:, :] = proj(0)
    gate_a = _silu(proj(4))
    t_idx = i * tm + lax.broadcasted_iota(jnp.int32, (tm, 1), 0)
    for g, w in enumerate(POOL_WINDOWS):
        cols = slice(g * POOL_GROUP_DIM, (g + 1) * POOL_GROUP_DIM)
        e = aext_ref[:, cols]
        acc = e
        span = 1
        while span < w:
            acc = acc + pltpu.roll(acc, span, axis=0)
            span *= 2
        cnt = jnp.minimum(t_idx + 1, w).astype(jnp.float32)
        pooled = acc[POOL_HALO:, :] / cnt - e[POOL_HALO:, :]
        y = _dot(pooled.astype(jnp.bfloat16), pw_ref[g]) * ps_ref[:, cols]
        ya_ref[:, cols] = (y * gate_a[:, cols]).astype(ya_ref.dtype)


def _in0_call(x, pos_f, g, w_in, pool_w, pool_scale, invf, m1, m2):
    B, S, _ = x.shape
    tm = TM_IN0
    tok = lambda b, i: (b, i, 0)
    const2 = lambda b, i: (0, 0)
    out = jax.ShapeDtypeStruct((B, S, 1024), jnp.bfloat16)
    return pl.pallas_call(
        _in0_kernel,
        out_shape=[out] * 5,
        grid=(B, S // tm),
        in_specs=[
            pl.BlockSpec((None, tm, D_MODEL), tok),
            pl.BlockSpec((None, tm, 1), tok),
            pl.BlockSpec((1, D_MODEL), const2),
            pl.BlockSpec((D_MODEL, IN_COLS), const2, pipeline_mode=pl.Buffered(1)),
            pl.BlockSpec((POOL_GROUPS, POOL_GROUP_DIM, POOL_GROUP_DIM), lambda b, i: (0, 0, 0)),
            pl.BlockSpec((1, POOL_WIDTH), const2),
            pl.BlockSpec((1, LANES), const2),
            pl.BlockSpec((1, LANES), const2),
            pl.BlockSpec((1, LANES), const2),
        ],
        out_specs=[pl.BlockSpec((None, tm, 1024), tok)] * 5,
        scratch_shapes=[pltpu.VMEM((POOL_HALO + tm, POOL_WIDTH), jnp.float32)],
        compiler_params=pltpu.CompilerParams(
            dimension_semantics=("arbitrary", "arbitrary"),
            vmem_limit_bytes=VMEM_LIMIT),
        name="l0_in_proj",
    )(x, pos_f, g, w_in, pool_w, pool_scale, invf, m1, m2)


def _attn_kernel(lam_init, q_ref, k_ref, v_ref, sgb_ref, lq1_ref, lk1_ref, lq2_ref, lk2_ref,
                 sub_ref, o_ref, qs_ref, m_ref, l_ref, acc_ref):
    qi = pl.program_id(2)
    tq = q_ref.shape[0]
    q = q_ref[...]
    lane = lax.broadcasted_iota(jnp.int32, (tq, LANES), 1)
    zero = jnp.zeros_like(q)
    qs_ref[0:tq, :] = jnp.where(lane < DIFF_QKDIM, q, zero)
    qs_ref[tq:, :] = jnp.where(lane >= DIFF_QKDIM, q, zero)
    m_ref[...] = jnp.full(m_ref.shape, -jnp.inf, jnp.float32)
    l_ref[...] = jnp.zeros(l_ref.shape, jnp.float32)
    acc_ref[...] = jnp.zeros(acc_ref.shape, jnp.float32)

    def step(c, masked):
        start = pl.multiple_of(c * TK, TK)
        kc = k_ref[pl.ds(start, TK), :]
        vc = v_ref[pl.ds(start, TK), :]
        s = lax.dot_general(qs_ref[...], kc, _NT, preferred_element_type=jnp.float32)
        if masked:
            row = lax.broadcasted_iota(jnp.int32, (2 * tq, TK), 0)
            col = lax.broadcasted_iota(jnp.int32, (2 * tq, TK), 1)
            qpos = qi * tq + jnp.where(row >= tq, row - tq, row)
            s = jnp.where(start + col <= qpos, s, -jnp.inf)
        m_old = m_ref[...]
        m_new = jnp.maximum(m_old, jnp.max(s, axis=-1, keepdims=True))
        alpha = jnp.exp(m_old - m_new)
        p = jnp.exp(s - m_new)
        l_ref[...] = alpha * l_ref[...] + jnp.sum(p, axis=-1, keepdims=True)
        acc_ref[...] = alpha * acc_ref[...] + _dot(p.astype(jnp.bfloat16), vc)
        m_ref[...] = m_new

    n_full = (qi * tq + 1) // TK

    def body(c, carry):
        step(c, False)
        return carry

    lax.fori_loop(0, n_full, body, 0)
    step(n_full, True)

    lam = (jnp.exp(jnp.sum(lq1_ref[...] * lk1_ref[...], axis=-1, keepdims=True))
           - jnp.exp(jnp.sum(lq2_ref[...] * lk2_ref[...], axis=-1, keepdims=True))
           + lam_init)
    o = (acc_ref[0:tq, :] / l_ref[0:tq, :]
         - lam * (acc_ref[tq:, :] / l_ref[tq:, :]))
    o = _rms_norm(o, sub_ref[...], SUBLN_EPS) * (1.0 - lam_init)
    o_ref[...] = (o * sgb_ref[...].astype(jnp.float32)).astype(o_ref.dtype)


def _attn_call(q, k, v, sgb, lq1, lk1, lq2, lk2, subln, lam_init):
    B, S, _ = q.shape
    qblk = lambda b, h, i: (b, i, h)
    kvblk = lambda b, h, i: (b, 0, h)
    const2 = lambda b, h, i: (0, 0)
    return pl.pallas_call(
        functools.partial(_attn_kernel, lam_init),
        out_shape=jax.ShapeDtypeStruct((B, S, ATTN_WIDTH), jnp.bfloat16),
        grid=(B, N_DIFF_HEADS, S // TQ),
        in_specs=[
            pl.BlockSpec((None, TQ, DIFF_VDIM), qblk),
            pl.BlockSpec((None, S, DIFF_VDIM), kvblk),
            pl.BlockSpec((None, S, DIFF_VDIM), kvblk),
            pl.BlockSpec((None, TQ, DIFF_VDIM), qblk),
            pl.BlockSpec((1, DIFF_QKDIM), const2),
            pl.BlockSpec((1, DIFF_QKDIM), const2),
            pl.BlockSpec((1, DIFF_QKDIM), const2),
            pl.BlockSpec((1, DIFF_QKDIM), const2),
            pl.BlockSpec((1, DIFF_VDIM), const2),
        ],
        out_specs=pl.BlockSpec((None, TQ, DIFF_VDIM), qblk),
        scratch_shapes=[
            pltpu.VMEM((2 * TQ, DIFF_VDIM), jnp.bfloat16),
            pltpu.VMEM((2 * TQ, 1), jnp.float32),
            pltpu.VMEM((2 * TQ, 1), jnp.float32),
            pltpu.VMEM((2 * TQ, DIFF_VDIM), jnp.float32),
        ],
        compiler_params=pltpu.CompilerParams(
            dimension_semantics=("arbitrary", "arbitrary", "arbitrary"),
            vmem_limit_bytes=VMEM_LIMIT),
        name="l0_diff_attn",
    )(q, k, v, sgb, lq1, lk1, lq2, lk2, subln)


def _out0_kernel(ya_ref, yb_ref, x_ref, w_ref, g_ref, o_ref):
    out = _dot(ya_ref[...], w_ref[0:POOL_WIDTH, :]) + _dot(yb_ref[...], w_ref[POOL_WIDTH:, :])
    o_ref[...] = x_ref[...] + _rms_norm(out, g_ref[...], EPS)


def _out0_call(ya, yb, x, w_out, g):
    n = x.shape[0]
    tm = TM_OUT0
    tok = lambda i: (i, 0)
    const2 = lambda i: (0, 0)
    return pl.pallas_call(
        _out0_kernel,
        out_shape=jax.ShapeDtypeStruct((n, D_MODEL), jnp.float32),
        grid=(n // tm,),
        in_specs=[
            pl.BlockSpec((tm, POOL_WIDTH), tok),
            pl.BlockSpec((tm, ATTN_WIDTH), tok),
            pl.BlockSpec((tm, D_MODEL), tok),
            pl.BlockSpec((D_INNER, D_MODEL), const2, pipeline_mode=pl.Buffered(1)),
            pl.BlockSpec((1, D_MODEL), const2),
        ],
        out_specs=pl.BlockSpec((tm, D_MODEL), tok),
        compiler_params=pltpu.CompilerParams(
            dimension_semantics=("arbitrary",),
            vmem_limit_bytes=VMEM_LIMIT),
        name="l0_out_proj",
    )(ya, yb, x, w_out, g)


def _l1_kernel(x_ref, g_ref, win_ref, lng_ref, lnb_ref, ws_ref, bs_ref, wout_ref, pg_ref,
               o_ref, u_ref, v_ref, y_ref):
    tm = x_ref.shape[0]
    x = x_ref[...]
    h = _rms_norm(x, g_ref[...], EPS).astype(jnp.bfloat16)
    nblk = D_INNER // 1024
    for j in range(nblk):
        cols = slice(j * 1024, (j + 1) * 1024)
        u_ref[:, cols] = _gelu_tanh(_dot(h, win_ref[:, j * 1024:(j + 1) * 1024]))
        v_ref[:, cols] = _gelu_tanh(_dot(h, win_ref[:, D_INNER + j * 1024:D_INNER + (j + 1) * 1024]))

    vf = v_ref[...]
    mu = jnp.mean(vf, axis=-1, keepdims=True)
    vc = vf - mu
    var = jnp.mean(vc * vc, axis=-1, keepdims=True)
    vn = (vc * lax.rsqrt(var + EPS) * lng_ref[...] + lnb_ref[...]).astype(jnp.bfloat16)

    tri_r = lax.broadcasted_iota(jnp.int32, (SGU_CHUNK, SGU_CHUNK), 0)
    tri_c = lax.broadcasted_iota(jnp.int32, (SGU_CHUNK, SGU_CHUNK), 1)
    tril = tri_c <= tri_r
    for g in range(SGU_GROUPS):
        cols = slice(g * SGU_GROUP_DIM, (g + 1) * SGU_GROUP_DIM)
        wg = jnp.where(tril, ws_ref[g], 0.0).astype(jnp.bfloat16)
        bias = bs_ref[:, g:g + 1]
        gate = _silu(_dot(h, win_ref[:, 2 * D_INNER + g * SGU_GROUP_DIM:
                                     2 * D_INNER + (g + 1) * SGU_GROUP_DIM]))
        for c in range(tm // SGU_CHUNK):
            rows = slice(c * SGU_CHUNK, (c + 1) * SGU_CHUNK)
            mixed = _dot(wg, vn[rows, cols]) + bias
            y_ref[rows, cols] = (u_ref[rows, cols] * mixed * gate[rows, :]).astype(y_ref.dtype)

    out = _dot(y_ref[...], wout_ref[...])
    o_ref[...] = x + _rms_norm(out, pg_ref[...], EPS)


def _l1_call(x, g, w_in, ln_g, ln_b, w_s, b_s_t, w_out, post_g):
    n = x.shape[0]
    tm = TM_L1
    tok = lambda i: (i, 0)
    const2 = lambda i: (0, 0)
    return pl.pallas_call(
        _l1_kernel,
        out_shape=jax.ShapeDtypeStruct((n, D_MODEL), jnp.float32),
        grid=(n // tm,),
        in_specs=[
            pl.BlockSpec((tm, D_MODEL), tok),
            pl.BlockSpec((1, D_MODEL), const2),
            pl.BlockSpec((D_MODEL, IN_COLS), const2, pipeline_mode=pl.Buffered(1)),
            pl.BlockSpec((1, D_INNER), const2),
            pl.BlockSpec((1, D_INNER), const2),
            pl.BlockSpec((SGU_GROUPS, SGU_CHUNK, SGU_CHUNK), lambda i: (0, 0, 0)),
            pl.BlockSpec((SGU_CHUNK, SGU_GROUPS), const2),
            pl.BlockSpec((D_INNER, D_MODEL), const2, pipeline_mode=pl.Buffered(1)),
            pl.BlockSpec((1, D_MODEL), const2),
        ],
        out_specs=pl.BlockSpec((tm, D_MODEL), tok),
        scratch_shapes=[
            pltpu.VMEM((tm, D_INNER), jnp.float32),
            pltpu.VMEM((tm, D_INNER), jnp.float32),
            pltpu.VMEM((tm, D_INNER), jnp.bfloat16),
        ],
        compiler_params=pltpu.CompilerParams(
            dimension_semantics=("arbitrary",),
            vmem_limit_bytes=VMEM_LIMIT),
        name="l1_sgu_layer",
    )(x, g, w_in, ln_g, ln_b, w_s, b_s_t, w_out, post_g)


def _rope_lane_tables():
    half = ROPE_DIM // 2
    inv_freq = jnp.power(ROPE_THETA, -jnp.arange(half, dtype=jnp.float32) * 2.0 / ROPE_DIM)
    d = jnp.arange(LANES) % DIFF_QKDIM
    invf = jnp.where(d < ROPE_DIM, inv_freq[d % half], 0.0).astype(jnp.float32)
    m1 = jnp.where(d < half, -1.0, 0.0).astype(jnp.float32)
    m2 = jnp.where((d >= half) & (d < ROPE_DIM), 1.0, 0.0).astype(jnp.float32)
    return invf[None, :], m1[None, :], m2[None, :]


def kernel(x, positions, pre_norm, post_norm, w_in, w_out, pool_w, pool_scale, lam_q1, lam_k1, lam_q2, lam_k2, diff_subln, sgu_ln_g, sgu_ln_b, sgu_w, sgu_b):
    B, S, D = x.shape
    bf16 = jnp.bfloat16
    w_in_b = w_in.astype(bf16)
    w_out_b = w_out.astype(bf16)
    invf, m1, m2 = _rope_lane_tables()
    pos_f = positions.astype(jnp.float32)[:, :, None]

    q, k, v, ya, sgb = _in0_call(x, pos_f, pre_norm[0:1], w_in_b[0], pool_w[0].astype(bf16),
                                 pool_scale[0:1], invf, m1, m2)
    lam_init = 0.8 - 0.6 * math.exp(-0.3 * 0)
    yb = _attn_call(q, k, v, sgb, lam_q1[0:1], lam_k1[0:1], lam_q2[0:1], lam_k2[0:1],
                    diff_subln[0:1], lam_init)
    n = B * S
    x1 = _out0_call(ya.reshape(n, POOL_WIDTH), yb.reshape(n, ATTN_WIDTH), x.reshape(n, D),
                    w_out_b[0], post_norm[0:1])

    x2 = _l1_call(x1, pre_norm[1:2], w_in_b[1], sgu_ln_g[0:1], sgu_ln_b[0:1], sgu_w[0],
                  sgu_b[0].T, w_out_b[1], post_norm[1:2])
    return x2.reshape(B, S, D)
```

```python
import functools
import math

import jax
import jax.numpy as jnp
from jax import lax
from jax.experimental import pallas as pl
from jax.experimental.pallas import tpu as pltpu

D_MODEL = 1024
D_INNER = 2048
IN_COLS = 3 * D_INNER
POOL_WIDTH = 1024
POOL_GROUPS = 4
POOL_GROUP_DIM = 256
POOL_WINDOWS = (2, 4, 8, 16)
POOL_HALO = 16
ATTN_WIDTH = 1024
N_DIFF_HEADS = 8
DIFF_VDIM = 128
DIFF_QKDIM = 64
ROPE_THETA = 500000.0
ROPE_DIM = 16
SGU_CHUNK = 128
SGU_GROUPS = 8
SGU_GROUP_DIM = 256
EPS = 1e-6
SUBLN_EPS = 1e-5
LANES = 128

TQ = 256
TK = 512
TM_IN0 = TK
TM_OUT0 = 512
TM_L1 = 256
VMEM_LIMIT = 56 * 1024 * 1024

_NT = (((1,), (1,)), ((), ()))


def _rms_norm(x, g, eps):
    return x * lax.rsqrt(jnp.mean(x * x, axis=-1, keepdims=True) + eps) * g


def _silu(x):
    return x / (1.0 + jnp.exp(-x))


def _gelu_tanh(x):
    c = math.sqrt(2.0 / math.pi)
    return 0.5 * x * (1.0 + jnp.tanh(c * (x + 0.044715 * (x * x * x))))


def _dot(a, b):
    return jnp.dot(a, b, preferred_element_type=jnp.float32)


def _in0_kernel(x_ref, pos_ref, g_ref, w_ref, pw_ref, ps_ref, invf_ref, m1_ref, m2_ref,
                q_ref, k_ref, vt_ref, ya_ref, sgb_ref, aext_ref):
    i = pl.program_id(1)
    tm = x_ref.shape[0]
    h = _rms_norm(x_ref[...], g_ref[...], EPS).astype(jnp.bfloat16)

    def proj(col):
        return _dot(h, w_ref[:, col * 1024:(col + 1) * 1024])

    ang = pos_ref[...] * invf_ref[...]
    cs = jnp.cos(ang)
    sn = jnp.sin(ang)
    s_lo = sn * m1_ref[...]
    s_hi = sn * m2_ref[...]
    half = ROPE_DIM // 2

    def rope_store(t, out_ref, scale):
        for c in range(ATTN_WIDTH // LANES):
            tc = t[:, c * LANES:(c + 1) * LANES]
            up = pltpu.roll(tc, LANES - half, axis=1)
            dn = pltpu.roll(tc, half, axis=1)
            r = tc * cs + up * s_lo + dn * s_hi
            if scale != 1.0:
                r = r * scale
            out_ref[:, c * LANES:(c + 1) * LANES] = r.astype(out_ref.dtype)

    rope_store(proj(1), q_ref, DIFF_QKDIM ** -0.5)
    rope_store(proj(2), k_ref, 1.0)
    vt_ref[...] = proj(3).T.astype(vt_ref.dtype)
    sgb_ref[...] = _silu(proj(5)).astype(sgb_ref.dtype)

    @pl.when(i == 0)
    def _():
        aext_ref[0:POOL_HALO, :] = jnp.zeros((POOL_HALO, POOL_WIDTH), jnp.float32)

    @pl.when(i > 0)
    def _():
        aext_ref[0:POOL_HALO, :] = aext_ref[tm:tm + POOL_HALO, :]

    aext_ref[POOL_HALO:, :] = proj(0)
    gate_a = _silu(proj(4))
    t_idx = i * tm + lax.broadcasted_iota(jnp.int32, (tm, 1), 0)
    for g, w in enumerate(POOL_WINDOWS):
        cols = slice(g * POOL_GROUP_DIM, (g + 1) * POOL_GROUP_DIM)
        e = aext_ref[:, cols]
        acc = e
        span = 1
        while span < w:
            acc = acc + pltpu.roll(acc, span, axis=0)
            span *= 2
        cnt = jnp.minimum(t_idx + 1, w).astype(jnp.float32)
        pooled = acc[POOL_HALO:, :] / cnt - e[POOL_HALO:, :]
        y = _dot(pooled.astype(jnp.bfloat16), pw_ref[g]) * ps_ref[:, cols]
        ya_ref[:, cols] = (y * gate_a[:, cols]).astype(ya_ref.dtype)


def _in0_call(x, pos_f, g, w_in, pool_w, pool_scale, invf, m1, m2):
    B, S, _ = x.shape
    tm = TM_IN0
    tok = lambda b, i: (b, i, 0)
    const2 = lambda b, i: (0, 0)
    out = jax.ShapeDtypeStruct((B, S, 1024), jnp.bfloat16)
    out_vt = jax.ShapeDtypeStruct((B, S // tm, ATTN_WIDTH, tm), jnp.bfloat16)
    spec = pl.BlockSpec((None, tm, 1024), tok)
    spec_vt = pl.BlockSpec((None, None, ATTN_WIDTH, tm), lambda b, i: (b, i, 0, 0))
    return pl.pallas_call(
        _in0_kernel,
        out_shape=[out, out, out_vt, out, out],
        grid=(B, S // tm),
        in_specs=[
            pl.BlockSpec((None, tm, D_MODEL), tok),
            pl.BlockSpec((None, tm, 1), tok),
            pl.BlockSpec((1, D_MODEL), const2),
            pl.BlockSpec((D_MODEL, IN_COLS), const2, pipeline_mode=pl.Buffered(1)),
            pl.BlockSpec((POOL_GROUPS, POOL_GROUP_DIM, POOL_GROUP_DIM), lambda b, i: (0, 0, 0)),
            pl.BlockSpec((1, POOL_WIDTH), const2),
            pl.BlockSpec((1, LANES), const2),
            pl.BlockSpec((1, LANES), const2),
            pl.BlockSpec((1, LANES), const2),
        ],
        out_specs=[spec, spec, spec_vt, spec, spec],
        scratch_shapes=[pltpu.VMEM((POOL_HALO + tm, POOL_WIDTH), jnp.float32)],
        compiler_params=pltpu.CompilerParams(
            dimension_semantics=("arbitrary", "arbitrary"),
            vmem_limit_bytes=VMEM_LIMIT),
        name="l0_in_proj",
    )(x, pos_f, g, w_in, pool_w, pool_scale, invf, m1, m2)


def _attn_kernel(lam_init, q_ref, k_ref, vt_ref, sgb_ref, lq1_ref, lk1_ref, lq2_ref, lk2_ref,
                 sub_ref, o_ref, qs_ref, m_ref, l_ref, acc_ref):
    qi = pl.program_id(2)
    tq = q_ref.shape[0]
    q = q_ref[...]
    lane = lax.broadcasted_iota(jnp.int32, (tq, LANES), 1)
    zero = jnp.zeros_like(q)
    qs_ref[0:tq, :] = jnp.where(lane < DIFF_QKDIM, q, zero)
    qs_ref[tq:, :] = jnp.where(lane >= DIFF_QKDIM, q, zero)
    m_ref[...] = jnp.full(m_ref.shape, -jnp.inf, jnp.float32)
    l_ref[...] = jnp.zeros(l_ref.shape, jnp.float32)
    acc_ref[...] = jnp.zeros(acc_ref.shape, jnp.float32)

    def step(c, masked):
        start = pl.multiple_of(c * TK, TK)
        kc = k_ref[pl.ds(start, TK), :]
        s = lax.dot_general(kc, qs_ref[...], _NT, preferred_element_type=jnp.float32)
        if masked:
            key = lax.broadcasted_iota(jnp.int32, (TK, 2 * tq), 0)
            col = lax.broadcasted_iota(jnp.int32, (TK, 2 * tq), 1)
            qpos = qi * tq + jnp.where(col >= tq, col - tq, col)
            s = jnp.where(start + key <= qpos, s, -jnp.inf)
        m_old = m_ref[...]
        m_new = jnp.maximum(m_old, jnp.max(s, axis=0, keepdims=True))
        alpha = jnp.exp(m_old - m_new)
        p = jnp.exp(s - m_new)
        l_ref[...] = alpha * l_ref[...] + jnp.sum(p, axis=0, keepdims=True)
        acc_ref[...] = alpha * acc_ref[...] + _dot(vt_ref[c], p.astype(jnp.bfloat16))
        m_ref[...] = m_new

    n_full = (qi * tq + 1) // TK

    def body(c, carry):
        step(c, False)
        return carry

    lax.fori_loop(0, n_full, body, 0)
    step(n_full, True)

    lam = (jnp.exp(jnp.sum(lq1_ref[...] * lk1_ref[...], axis=-1, keepdims=True))
           - jnp.exp(jnp.sum(lq2_ref[...] * lk2_ref[...], axis=-1, keepdims=True))
           + lam_init)
    ot = (acc_ref[:, 0:tq] / l_ref[:, 0:tq]
          - lam * (acc_ref[:, tq:] / l_ref[:, tq:]))
    ms = jnp.mean(ot * ot, axis=0, keepdims=True)
    ot = ot * lax.rsqrt(ms + SUBLN_EPS) * sub_ref[...] * (1.0 - lam_init)
    o_ref[...] = (ot.T * sgb_ref[...].astype(jnp.float32)).astype(o_ref.dtype)


def _attn_call(q, k, vt, sgb, lq1, lk1, lq2, lk2, subln_col, lam_init):
    B, S, _ = q.shape
    qblk = lambda b, h, i: (b, i, h)
    const2 = lambda b, h, i: (0, 0)
    return pl.pallas_call(
        functools.partial(_attn_kernel, lam_init),
        out_shape=jax.ShapeDtypeStruct((B, S, ATTN_WIDTH), jnp.bfloat16),
        grid=(B, N_DIFF_HEADS, S // TQ),
        in_specs=[
            pl.BlockSpec((None, TQ, DIFF_VDIM), qblk),
            pl.BlockSpec((None, S, DIFF_VDIM), lambda b, h, i: (b, 0, h)),
            pl.BlockSpec((None, S // TK, DIFF_VDIM, TK), lambda b, h, i: (b, 0, h, 0)),
            pl.BlockSpec((None, TQ, DIFF_VDIM), qblk),
            pl.BlockSpec((1, DIFF_QKDIM), const2),
            pl.BlockSpec((1, DIFF_QKDIM), const2),
            pl.BlockSpec((1, DIFF_QKDIM), const2),
            pl.BlockSpec((1, DIFF_QKDIM), const2),
            pl.BlockSpec((DIFF_VDIM, 1), const2),
        ],
        out_specs=pl.BlockSpec((None, TQ, DIFF_VDIM), qblk),
        scratch_shapes=[
            pltpu.VMEM((2 * TQ, DIFF_VDIM), jnp.bfloat16),
            pltpu.VMEM((1, 2 * TQ), jnp.float32),
            pltpu.VMEM((1, 2 * TQ), jnp.float32),
            pltpu.VMEM((DIFF_VDIM, 2 * TQ), jnp.float32),
        ],
        compiler_params=pltpu.CompilerParams(
            dimension_semantics=("arbitrary", "arbitrary", "arbitrary"),
            vmem_limit_bytes=VMEM_LIMIT),
        name="l0_diff_attn",
    )(q, k, vt, sgb, lq1, lk1, lq2, lk2, subln_col)


def _out0_kernel(ya_ref, yb_ref, x_ref, w_ref, g_ref, o_ref):
    out = _dot(ya_ref[...], w_ref[0:POOL_WIDTH, :]) + _dot(yb_ref[...], w_ref[POOL_WIDTH:, :])
    o_ref[...] = x_ref[...] + _rms_norm(out, g_ref[...], EPS)


def _out0_call(ya, yb, x, w_out, g):
    n = x.shape[0]
    tm = TM_OUT0
    tok = lambda i: (i, 0)
    const2 = lambda i: (0, 0)
    return pl.pallas_call(
        _out0_kernel,
        out_shape=jax.ShapeDtypeStruct((n, D_MODEL), jnp.float32),
        grid=(n // tm,),
        in_specs=[
            pl.BlockSpec((tm, POOL_WIDTH), tok),
            pl.BlockSpec((tm, ATTN_WIDTH), tok),
            pl.BlockSpec((tm, D_MODEL), tok),
            pl.BlockSpec((D_INNER, D_MODEL), const2, pipeline_mode=pl.Buffered(1)),
            pl.BlockSpec((1, D_MODEL), const2),
        ],
        out_specs=pl.BlockSpec((tm, D_MODEL), tok),
        compiler_params=pltpu.CompilerParams(
            dimension_semantics=("arbitrary",),
            vmem_limit_bytes=VMEM_LIMIT),
        name="l0_out_proj",
    )(ya, yb, x, w_out, g)


def _l1_kernel(x_ref, g_ref, win_ref, lng_ref, lnb_ref, ws_ref, bs_ref, wout_ref, pg_ref,
               o_ref, u_ref, v_ref, y_ref):
    tm = x_ref.shape[0]
    x = x_ref[...]
    h = _rms_norm(x, g_ref[...], EPS).astype(jnp.bfloat16)
    nblk = D_INNER // 1024
    for j in range(nblk):
        cols = slice(j * 1024, (j + 1) * 1024)
        u_ref[:, cols] = _gelu_tanh(_dot(h, win_ref[:, j * 1024:(j + 1) * 1024]))
        v_ref[:, cols] = _gelu_tanh(_dot(h, win_ref[:, D_INNER + j * 1024:D_INNER + (j + 1) * 1024]))

    vf = v_ref[...]
    mu = jnp.mean(vf, axis=-1, keepdims=True)
    vc = vf - mu
    var = jnp.mean(vc * vc, axis=-1, keepdims=True)
    vn = (vc * lax.rsqrt(var + EPS) * lng_ref[...] + lnb_ref[...]).astype(jnp.bfloat16)

    tri_r = lax.broadcasted_iota(jnp.int32, (SGU_CHUNK, SGU_CHUNK), 0)
    tri_c = lax.broadcasted_iota(jnp.int32, (SGU_CHUNK, SGU_CHUNK), 1)
    tril = tri_c <= tri_r
    for g in range(SGU_GROUPS):
        cols = slice(g * SGU_GROUP_DIM, (g + 1) * SGU_GROUP_DIM)
        wg = jnp.where(tril, ws_ref[g], 0.0).astype(jnp.bfloat16)
        bias = bs_ref[:, g:g + 1]
        gate = _silu(_dot(h, win_ref[:, 2 * D_INNER + g * SGU_GROUP_DIM:
                                     2 * D_INNER + (g + 1) * SGU_GROUP_DIM]))
        for c in range(tm // SGU_CHUNK):
            rows = slice(c * SGU_CHUNK, (c + 1) * SGU_CHUNK)
            mixed = _dot(wg, vn[rows, cols]) + bias
            y_ref[rows, cols] = (u_ref[rows, cols] * mixed * gate[rows, :]).astype(y_ref.dtype)

    out = _dot(y_ref[...], wout_ref[...])
    o_ref[...] = x + _rms_norm(out, pg_ref[...], EPS)


def _l1_call(x, g, w_in, ln_g, ln_b, w_s, b_s_t, w_out, post_g):
    n = x.shape[0]
    tm = TM_L1
    tok = lambda i: (i, 0)
    const2 = lambda i: (0, 0)
    return pl.pallas_call(
        _l1_kernel,
        out_shape=jax.ShapeDtypeStruct((n, D_MODEL), jnp.float32),
        grid=(n // tm,),
        in_specs=[
            pl.BlockSpec((tm, D_MODEL), tok),
            pl.BlockSpec((1, D_MODEL), const2),
            pl.BlockSpec((D_MODEL, IN_COLS), const2, pipeline_mode=pl.Buffered(1)),
            pl.BlockSpec((1, D_INNER), const2),
            pl.BlockSpec((1, D_INNER), const2),
            pl.BlockSpec((SGU_GROUPS, SGU_CHUNK, SGU_CHUNK), lambda i: (0, 0, 0)),
            pl.BlockSpec((SGU_CHUNK, SGU_GROUPS), const2),
            pl.BlockSpec((D_INNER, D_MODEL), const2, pipeline_mode=pl.Buffered(1)),
            pl.BlockSpec((1, D_MODEL), const2),
        ],
        out_specs=pl.BlockSpec((tm, D_MODEL), tok),
        scratch_shapes=[
            pltpu.VMEM((tm, D_INNER), jnp.float32),
            pltpu.VMEM((tm, D_INNER), jnp.float32),
            pltpu.VMEM((tm, D_INNER), jnp.bfloat16),
        ],
        compiler_params=pltpu.CompilerParams(
            dimension_semantics=("arbitrary",),
            vmem_limit_bytes=VMEM_LIMIT),
        name="l1_sgu_layer",
    )(x, g, w_in, ln_g, ln_b, w_s, b_s_t, w_out, post_g)


def _rope_lane_tables():
    half = ROPE_DIM // 2
    inv_freq = jnp.power(ROPE_THETA, -jnp.arange(half, dtype=jnp.float32) * 2.0 / ROPE_DIM)
    d = jnp.arange(LANES) % DIFF_QKDIM
    invf = jnp.where(d < ROPE_DIM, inv_freq[d % half], 0.0).astype(jnp.float32)
    m1 = jnp.where(d < half, -1.0, 0.0).astype(jnp.float32)
    m2 = jnp.where((d >= half) & (d < ROPE_DIM), 1.0, 0.0).astype(jnp.float32)
    return invf[None, :], m1[None, :], m2[None, :]


def kernel(x, positions, pre_norm, post_norm, w_in, w_out, pool_w, pool_scale, lam_q1, lam_k1, lam_q2, lam_k2, diff_subln, sgu_ln_g, sgu_ln_b, sgu_w, sgu_b):
    B, S, D = x.shape
    bf16 = jnp.bfloat16
    w_in_b = w_in.astype(bf16)
    w_out_b = w_out.astype(bf16)
    invf, m1, m2 = _rope_lane_tables()
    pos_f = positions.astype(jnp.float32)[:, :, None]

    q, k, vt, ya, sgb = _in0_call(x, pos_f, pre_norm[0:1], w_in_b[0], pool_w[0].astype(bf16),
                                 pool_scale[0:1], invf, m1, m2)
    lam_init = 0.8 - 0.6 * math.exp(-0.3 * 0)
    yb = _attn_call(q, k, vt, sgb, lam_q1[0:1], lam_k1[0:1], lam_q2[0:1], lam_k2[0:1],
                    diff_subln[0].reshape(DIFF_VDIM, 1), lam_init)
    n = B * S
    x1 = _out0_call(ya.reshape(n, POOL_WIDTH), yb.reshape(n, ATTN_WIDTH), x.reshape(n, D),
                    w_out_b[0], post_norm[0:1])

    x2 = _l1_call(x1, pre_norm[1:2], w_in_b[1], sgu_ln_g[0:1], sgu_ln_b[0:1], sgu_w[0],
                  sgu_b[0].T, w_out_b[1], post_norm[1:2])
    return x2.reshape(B, S, D)
```

```python
import functools
import math

import jax
import jax.numpy as jnp
from jax import lax
from jax.experimental import pallas as pl
from jax.experimental.pallas import tpu as pltpu

D_MODEL = 1024
D_INNER = 2048
IN_COLS = 3 * D_INNER
POOL_WIDTH = 1024
POOL_GROUPS = 4
POOL_GROUP_DIM = 256
POOL_WINDOWS = (2, 4, 8, 16)
POOL_HALO = 16
ATTN_WIDTH = 1024
N_DIFF_HEADS = 8
DIFF_VDIM = 128
DIFF_QKDIM = 64
ROPE_THETA = 500000.0
ROPE_DIM = 16
SGU_CHUNK = 128
SGU_GROUPS = 8
SGU_GROUP_DIM = 256
EPS = 1e-6
SUBLN_EPS = 1e-5
LANES = 128
LOG2_E = math.log2(math.e)

TQ = 256
TK = 512
TM_IN0 = TK
TM_OUT0 = 512
TM_L1 = 256
VMEM_LIMIT = 56 * 1024 * 1024

_NT = (((1,), (1,)), ((), ()))


def _rms_norm(x, g, eps):
    return x * lax.rsqrt(jnp.mean(x * x, axis=-1, keepdims=True) + eps) * g


def _silu(x):
    return x / (1.0 + jnp.exp(-x))


def _gelu_tanh(x):
    c = math.sqrt(2.0 / math.pi)
    return 0.5 * x * (1.0 + jnp.tanh(c * (x + 0.044715 * (x * x * x))))


def _dot(a, b):
    return jnp.dot(a, b, preferred_element_type=jnp.float32)


def _in0_kernel(x_ref, pos_ref, g_ref, w_ref, pw_ref, ps_ref, invf_ref, m1_ref, m2_ref,
                q_ref, k_ref, vt_ref, ya_ref, sgb_ref, aext_ref):
    i = pl.program_id(1)
    tm = x_ref.shape[0]
    h = _rms_norm(x_ref[...], g_ref[...], EPS).astype(jnp.bfloat16)

    def proj(col):
        return _dot(h, w_ref[:, col * 1024:(col + 1) * 1024])

    ang = pos_ref[...] * invf_ref[...]
    cs = jnp.cos(ang)
    sn = jnp.sin(ang)
    s_lo = sn * m1_ref[...]
    s_hi = sn * m2_ref[...]
    half = ROPE_DIM // 2

    def rope_store(t, out_ref, scale):
        for c in range(ATTN_WIDTH // LANES):
            tc = t[:, c * LANES:(c + 1) * LANES]
            up = pltpu.roll(tc, LANES - half, axis=1)
            dn = pltpu.roll(tc, half, axis=1)
            r = tc * cs + up * s_lo + dn * s_hi
            if scale != 1.0:
                r = r * scale
            out_ref[:, c * LANES:(c + 1) * LANES] = r.astype(out_ref.dtype)

    rope_store(proj(1), q_ref, DIFF_QKDIM ** -0.5 * LOG2_E)
    rope_store(proj(2), k_ref, 1.0)
    vt_ref[...] = proj(3).T.astype(vt_ref.dtype)
    sgb_ref[...] = _silu(proj(5)).astype(sgb_ref.dtype)

    @pl.when(i == 0)
    def _():
        aext_ref[0:POOL_HALO, :] = jnp.zeros((POOL_HALO, POOL_WIDTH), jnp.float32)

    @pl.when(i > 0)
    def _():
        aext_ref[0:POOL_HALO, :] = aext_ref[tm:tm + POOL_HALO, :]

    aext_ref[POOL_HALO:, :] = proj(0)
    gate_a = _silu(proj(4))
    t_idx = i * tm + lax.broadcasted_iota(jnp.int32, (tm, 1), 0)
    for g, w in enumerate(POOL_WINDOWS):
        cols = slice(g * POOL_GROUP_DIM, (g + 1) * POOL_GROUP_DIM)
        e = aext_ref[:, cols]
        acc = e
        span = 1
        while span < w:
            acc = acc + pltpu.roll(acc, span, axis=0)
            span *= 2
        cnt = jnp.minimum(t_idx + 1, w).astype(jnp.float32)
        pooled = acc[POOL_HALO:, :] / cnt - e[POOL_HALO:, :]
        y = _dot(pooled.astype(jnp.bfloat16), pw_ref[g]) * ps_ref[:, cols]
        ya_ref[:, cols] = (y * gate_a[:, cols]).astype(ya_ref.dtype)


def _in0_call(x, pos_f, g, w_in, pool_w, pool_scale, invf, m1, m2):
    B, S, _ = x.shape
    tm = TM_IN0
    tok = lambda b, i: (b, i, 0)
    const2 = lambda b, i: (0, 0)
    out = jax.ShapeDtypeStruct((B, S, 1024), jnp.bfloat16)
    out_vt = jax.ShapeDtypeStruct((B, S // tm, ATTN_WIDTH, tm), jnp.bfloat16)
    spec = pl.BlockSpec((None, tm, 1024), tok)
    spec_vt = pl.BlockSpec((None, None, ATTN_WIDTH, tm), lambda b, i: (b, i, 0, 0))
    return pl.pallas_call(
        _in0_kernel,
        out_shape=[out, out, out_vt, out, out],
        grid=(B, S // tm),
        in_specs=[
            pl.BlockSpec((None, tm, D_MODEL), tok),
            pl.BlockSpec((None, tm, 1), tok),
            pl.BlockSpec((1, D_MODEL), const2),
            pl.BlockSpec((D_MODEL, IN_COLS), const2, pipeline_mode=pl.Buffered(1)),
            pl.BlockSpec((POOL_GROUPS, POOL_GROUP_DIM, POOL_GROUP_DIM), lambda b, i: (0, 0, 0)),
            pl.BlockSpec((1, POOL_WIDTH), const2),
            pl.BlockSpec((1, LANES), const2),
            pl.BlockSpec((1, LANES), const2),
            pl.BlockSpec((1, LANES), const2),
        ],
        out_specs=[spec, spec, spec_vt, spec, spec],
        scratch_shapes=[pltpu.VMEM((POOL_HALO + tm, POOL_WIDTH), jnp.float32)],
        compiler_params=pltpu.CompilerParams(
            dimension_semantics=("arbitrary", "arbitrary"),
            vmem_limit_bytes=VMEM_LIMIT),
        name="l0_in_proj",
    )(x, pos_f, g, w_in, pool_w, pool_scale, invf, m1, m2)


def _attn_kernel(lam_init, q_ref, k_ref, vt_ref, sgb_ref, lq1_ref, lk1_ref, lq2_ref, lk2_ref,
                 sub_ref, o_ref, qs_ref, m_ref, l_ref, acc_ref, s_ref, cm_ref):
    qi = pl.program_id(2)
    tq = q_ref.shape[0]
    q = q_ref[...]
    lane = lax.broadcasted_iota(jnp.int32, (tq, LANES), 1)
    zero = jnp.zeros_like(q)
    qs_ref[0:tq, :] = jnp.where(lane < DIFF_QKDIM, q, zero)
    qs_ref[tq:, :] = jnp.where(lane >= DIFF_QKDIM, q, zero)
    m_ref[...] = jnp.full(m_ref.shape, -jnp.inf, jnp.float32)
    l_ref[...] = jnp.zeros(l_ref.shape, jnp.float32)
    acc_ref[...] = jnp.zeros(acc_ref.shape, jnp.float32)

    def scores(c, buf, masked):
        start = pl.multiple_of(c * TK, TK)
        kc = k_ref[pl.ds(start, TK), :]
        s = lax.dot_general(kc, qs_ref[...], _NT, preferred_element_type=jnp.float32)
        if masked:
            key = lax.broadcasted_iota(jnp.int32, (TK, 2 * tq), 0)
            col = lax.broadcasted_iota(jnp.int32, (TK, 2 * tq), 1)
            qpos = qi * tq + jnp.where(col >= tq, col - tq, col)
            s = jnp.where(start + key <= qpos, s, -jnp.inf)
        s_ref[buf] = s
        cm_ref[buf] = jnp.max(s, axis=0, keepdims=True)

    def accumulate(c, buf):
        m_old = m_ref[...]
        m_new = jnp.maximum(m_old, cm_ref[buf])
        alpha = jnp.exp2(m_old - m_new)
        p = jnp.exp2(s_ref[buf] - m_new)
        l_ref[...] = alpha * l_ref[...] + jnp.sum(p, axis=0, keepdims=True)
        acc_ref[...] = alpha * acc_ref[...] + _dot(vt_ref[c], p.astype(jnp.bfloat16))
        m_ref[...] = m_new

    n = (qi * tq + tq - 1) // TK + 1
    odd = n % 2 == 1

    @pl.when(n == 1)
    def _():
        scores(0, 0, True)

    @pl.when(n > 1)
    def _():
        scores(0, 0, False)

    def body(j, carry):
        c = 2 * j + 1
        scores(c, 1, False)
        accumulate(c - 1, 0)
        scores(c + 1, 0, False)
        accumulate(c, 1)
        return carry

    lax.fori_loop(0, (n - 2) // 2, body, 0)

    @pl.when(jnp.logical_not(odd))
    def _():
        scores(n - 1, 1, True)
        accumulate(n - 2, 0)
        accumulate(n - 1, 1)

    @pl.when(jnp.logical_and(odd, n > 1))
    def _():
        scores(n - 2, 1, False)
        accumulate(n - 3, 0)
        scores(n - 1, 0, True)
        accumulate(n - 2, 1)

    @pl.when(odd)
    def _():
        accumulate(n - 1, 0)

    lam = (jnp.exp(jnp.sum(lq1_ref[...] * lk1_ref[...], axis=-1, keepdims=True))
           - jnp.exp(jnp.sum(lq2_ref[...] * lk2_ref[...], axis=-1, keepdims=True))
           + lam_init)
    ot = (acc_ref[:, 0:tq] / l_ref[:, 0:tq]
          - lam * (acc_ref[:, tq:] / l_ref[:, tq:]))
    ms = jnp.mean(ot * ot, axis=0, keepdims=True)
    ot = ot * lax.rsqrt(ms + SUBLN_EPS) * sub_ref[...] * (1.0 - lam_init)
    o_ref[...] = (ot.T * sgb_ref[...].astype(jnp.float32)).astype(o_ref.dtype)


def _attn_call(q, k, vt, sgb, lq1, lk1, lq2, lk2, subln_col, lam_init):
    B, S, _ = q.shape
    qblk = lambda b, h, i: (b, i, h)
    const2 = lambda b, h, i: (0, 0)
    return pl.pallas_call(
        functools.partial(_attn_kernel, lam_init),
        out_shape=jax.ShapeDtypeStruct((B, S, ATTN_WIDTH), jnp.bfloat16),
        grid=(B, N_DIFF_HEADS, S // TQ),
        in_specs=[
            pl.BlockSpec((None, TQ, DIFF_VDIM), qblk),
            pl.BlockSpec((None, S, DIFF_VDIM), lambda b, h, i: (b, 0, h)),
            pl.BlockSpec((None, S // TK, DIFF_VDIM, TK), lambda b, h, i: (b, 0, h, 0)),
            pl.BlockSpec((None, TQ, DIFF_VDIM), qblk),
            pl.BlockSpec((1, DIFF_QKDIM), const2),
            pl.BlockSpec((1, DIFF_QKDIM), const2),
            pl.BlockSpec((1, DIFF_QKDIM), const2),
            pl.BlockSpec((1, DIFF_QKDIM), const2),
            pl.BlockSpec((DIFF_VDIM, 1), const2),
        ],
        out_specs=pl.BlockSpec((None, TQ, DIFF_VDIM), qblk),
        scratch_shapes=[
            pltpu.VMEM((2 * TQ, DIFF_VDIM), jnp.bfloat16),
            pltpu.VMEM((1, 2 * TQ), jnp.float32),
            pltpu.VMEM((1, 2 * TQ), jnp.float32),
            pltpu.VMEM((DIFF_VDIM, 2 * TQ), jnp.float32),
            pltpu.VMEM((2, TK, 2 * TQ), jnp.float32),
            pltpu.VMEM((2, 1, 2 * TQ), jnp.float32),
        ],
        compiler_params=pltpu.CompilerParams(
            dimension_semantics=("arbitrary", "arbitrary", "arbitrary"),
            vmem_limit_bytes=VMEM_LIMIT),
        name="l0_diff_attn",
    )(q, k, vt, sgb, lq1, lk1, lq2, lk2, subln_col)


def _out0_kernel(ya_ref, yb_ref, x_ref, w_ref, g_ref, o_ref):
    out = _dot(ya_ref[...], w_ref[0:POOL_WIDTH, :]) + _dot(yb_ref[...], w_ref[POOL_WIDTH:, :])
    o_ref[...] = x_ref[...] + _rms_norm(out, g_ref[...], EPS)


def _out0_call(ya, yb, x, w_out, g):
    n = x.shape[0]
    tm = TM_OUT0
    tok = lambda i: (i, 0)
    const2 = lambda i: (0, 0)
    return pl.pallas_call(
        _out0_kernel,
        out_shape=jax.ShapeDtypeStruct((n, D_MODEL), jnp.float32),
        grid=(n // tm,),
        in_specs=[
            pl.BlockSpec((tm, POOL_WIDTH), tok),
            pl.BlockSpec((tm, ATTN_WIDTH), tok),
            pl.BlockSpec((tm, D_MODEL), tok),
            pl.BlockSpec((D_INNER, D_MODEL), const2, pipeline_mode=pl.Buffered(1)),
            pl.BlockSpec((1, D_MODEL), const2),
        ],
        out_specs=pl.BlockSpec((tm, D_MODEL), tok),
        compiler_params=pltpu.CompilerParams(
            dimension_semantics=("arbitrary",),
            vmem_limit_bytes=VMEM_LIMIT),
        name="l0_out_proj",
    )(ya, yb, x, w_out, g)


def _l1_kernel(x_ref, g_ref, win_ref, lng_ref, lnb_ref, ws_ref, bs_ref, wout_ref, pg_ref,
               o_ref, u_ref, v_ref, y_ref):
    tm = x_ref.shape[0]
    x = x_ref[...]
    h = _rms_norm(x, g_ref[...], EPS).astype(jnp.bfloat16)
    nblk = D_INNER // 1024
    for j in range(nblk):
        cols = slice(j * 1024, (j + 1) * 1024)
        u_ref[:, cols] = _gelu_tanh(_dot(h, win_ref[:, j * 1024:(j + 1) * 1024]))
        v_ref[:, cols] = _gelu_tanh(_dot(h, win_ref[:, D_INNER + j * 1024:D_INNER + (j + 1) * 1024]))

    vf = v_ref[...]
    mu = jnp.mean(vf, axis=-1, keepdims=True)
    vc = vf - mu
    var = jnp.mean(vc * vc, axis=-1, keepdims=True)
    vn = (vc * lax.rsqrt(var + EPS) * lng_ref[...] + lnb_ref[...]).astype(jnp.bfloat16)

    tri_r = lax.broadcasted_iota(jnp.int32, (SGU_CHUNK, SGU_CHUNK), 0)
    tri_c = lax.broadcasted_iota(jnp.int32, (SGU_CHUNK, SGU_CHUNK), 1)
    tril = tri_c <= tri_r
    for g in range(SGU_GROUPS):
        cols = slice(g * SGU_GROUP_DIM, (g + 1) * SGU_GROUP_DIM)
        wg = jnp.where(tril, ws_ref[g], 0.0).astype(jnp.bfloat16)
        bias = bs_ref[:, g:g + 1]
        gate = _silu(_dot(h, win_ref[:, 2 * D_INNER + g * SGU_GROUP_DIM:
                                     2 * D_INNER + (g + 1) * SGU_GROUP_DIM]))
        for c in range(tm // SGU_CHUNK):
            rows = slice(c * SGU_CHUNK, (c + 1) * SGU_CHUNK)
            mixed = _dot(wg, vn[rows, cols]) + bias
            y_ref[rows, cols] = (u_ref[rows, cols] * mixed * gate[rows, :]).astype(y_ref.dtype)

    out = _dot(y_ref[...], wout_ref[...])
    o_ref[...] = x + _rms_norm(out, pg_ref[...], EPS)


def _l1_call(x, g, w_in, ln_g, ln_b, w_s, b_s_t, w_out, post_g):
    n = x.shape[0]
    tm = TM_L1
    tok = lambda i: (i, 0)
    const2 = lambda i: (0, 0)
    return pl.pallas_call(
        _l1_kernel,
        out_shape=jax.ShapeDtypeStruct((n, D_MODEL), jnp.float32),
        grid=(n // tm,),
        in_specs=[
            pl.BlockSpec((tm, D_MODEL), tok),
            pl.BlockSpec((1, D_MODEL), const2),
            pl.BlockSpec((D_MODEL, IN_COLS), const2, pipeline_mode=pl.Buffered(1)),
            pl.BlockSpec((1, D_INNER), const2),
            pl.BlockSpec((1, D_INNER), const2),
            pl.BlockSpec((SGU_GROUPS, SGU_CHUNK, SGU_CHUNK), lambda i: (0, 0, 0)),
            pl.BlockSpec((SGU_CHUNK, SGU_GROUPS), const2),
            pl.BlockSpec((D_INNER, D_MODEL), const2, pipeline_mode=pl.Buffered(1)),
            pl.BlockSpec((1, D_MODEL), const2),
        ],
        out_specs=pl.BlockSpec((tm, D_MODEL), tok),
        scratch_shapes=[
            pltpu.VMEM((tm, D_INNER), jnp.float32),
            pltpu.VMEM((tm, D_INNER), jnp.float32),
            pltpu.VMEM((tm, D_INNER), jnp.bfloat16),
        ],
        compiler_params=pltpu.CompilerParams(
            dimension_semantics=("arbitrary",),
            vmem_limit_bytes=VMEM_LIMIT),
        name="l1_sgu_layer",
    )(x, g, w_in, ln_g, ln_b, w_s, b_s_t, w_out, post_g)


def _rope_lane_tables():
    half = ROPE_DIM // 2
    inv_freq = jnp.power(ROPE_THETA, -jnp.arange(half, dtype=jnp.float32) * 2.0 / ROPE_DIM)
    d = jnp.arange(LANES) % DIFF_QKDIM
    invf = jnp.where(d < ROPE_DIM, inv_freq[d % half], 0.0).astype(jnp.float32)
    m1 = jnp.where(d < half, -1.0, 0.0).astype(jnp.float32)
    m2 = jnp.where((d >= half) & (d < ROPE_DIM), 1.0, 0.0).astype(jnp.float32)
    return invf[None, :], m1[None, :], m2[None, :]


def kernel(x, positions, pre_norm, post_norm, w_in, w_out, pool_w, pool_scale, lam_q1, lam_k1, lam_q2, lam_k2, diff_subln, sgu_ln_g, sgu_ln_b, sgu_w, sgu_b):
    B, S, D = x.shape
    bf16 = jnp.bfloat16
    w_in_b = w_in.astype(bf16)
    w_out_b = w_out.astype(bf16)
    invf, m1, m2 = _rope_lane_tables()
    pos_f = positions.astype(jnp.float32)[:, :, None]

    q, k, vt, ya, sgb = _in0_call(x, pos_f, pre_norm[0:1], w_in_b[0], pool_w[0].astype(bf16),
                                 pool_scale[0:1], invf, m1, m2)
    lam_init = 0.8 - 0.6 * math.exp(-0.3 * 0)
    yb = _attn_call(q, k, vt, sgb, lam_q1[0:1], lam_k1[0:1], lam_q2[0:1], lam_k2[0:1],
                    diff_subln[0].reshape(DIFF_VDIM, 1), lam_init)
    n = B * S
    x1 = _out0_call(ya.reshape(n, POOL_WIDTH), yb.reshape(n, ATTN_WIDTH), x.reshape(n, D),
                    w_out_b[0], post_norm[0:1])

    x2 = _l1_call(x1, pre_norm[1:2], w_in_b[1], sgu_ln_g[0:1], sgu_ln_b[0:1], sgu_w[0],
                  sgu_b[0].T, w_out_b[1], post_norm[1:2])
    return x2.reshape(B, S, D)
```

```python
import functools
import math

import jax
import jax.numpy as jnp
import numpy as np
from jax import lax
from jax.experimental import pallas as pl
from jax.experimental.pallas import tpu as pltpu

D_MODEL = 1024
D_INNER = 2048
IN_COLS = 3 * D_INNER
POOL_WIDTH = 1024
POOL_GROUPS = 4
POOL_GROUP_DIM = 256
POOL_WINDOWS = (2, 4, 8, 16)
POOL_HALO = 16
ATTN_WIDTH = 1024
N_DIFF_HEADS = 8
DIFF_VDIM = 128
DIFF_QKDIM = 64
ROPE_THETA = 500000.0
ROPE_DIM = 16
SGU_CHUNK = 128
SGU_GROUPS = 8
SGU_GROUP_DIM = 256
EPS = 1e-6
SUBLN_EPS = 1e-5
LANES = 128
LOG2_E = math.log2(math.e)

TQ = 256
TK = 512
TM_IN0 = TK
VT_PAD = 16
PIPE_BUFS = 3
PIPE_UNROLL = 6
assert PIPE_UNROLL % PIPE_BUFS == 0
TM_OUT0 = 512
TM_L1 = 256
VMEM_LIMIT = 56 * 1024 * 1024

_NT = (((1,), (1,)), ((), ()))


def _rms_norm(x, g, eps):
    return x * lax.rsqrt(jnp.mean(x * x, axis=-1, keepdims=True) + eps) * g


def _silu(x):
    return x / (1.0 + jnp.exp(-x))


def _gelu_tanh(x):
    c = math.sqrt(2.0 / math.pi)
    return 0.5 * x * (1.0 + jnp.tanh(c * (x + 0.044715 * (x * x * x))))


def _dot(a, b):
    return jnp.dot(a, b, preferred_element_type=jnp.float32)


def _in0_kernel(x_ref, pos_ref, g_ref, w_ref, pw_ref, ps_ref, invf_ref, m1_ref, m2_ref,
                q_ref, k_ref, vt_ref, ya_ref, sgb_ref, aext_ref):
    i = pl.program_id(1)
    tm = x_ref.shape[0]
    h = _rms_norm(x_ref[...], g_ref[...], EPS).astype(jnp.bfloat16)

    def proj(col):
        return _dot(h, w_ref[:, col * 1024:(col + 1) * 1024])

    ang = pos_ref[...] * invf_ref[...]
    cs = jnp.cos(ang)
    sn = jnp.sin(ang)
    s_lo = sn * m1_ref[...]
    s_hi = sn * m2_ref[...]
    half = ROPE_DIM // 2

    def rope_store(t, out_ref, scale):
        for c in range(ATTN_WIDTH // LANES):
            tc = t[:, c * LANES:(c + 1) * LANES]
            up = pltpu.roll(tc, LANES - half, axis=1)
            dn = pltpu.roll(tc, half, axis=1)
            r = tc * cs + up * s_lo + dn * s_hi
            if scale != 1.0:
                r = r * scale
            out_ref[:, c * LANES:(c + 1) * LANES] = r.astype(out_ref.dtype)

    rope_store(proj(1), q_ref, DIFF_QKDIM ** -0.5 * LOG2_E)
    rope_store(proj(2), k_ref, 1.0)
    vt_ref[...] = proj(3).T.astype(vt_ref.dtype)
    sgb_ref[...] = _silu(proj(5)).astype(sgb_ref.dtype)

    @pl.when(i == 0)
    def _():
        aext_ref[0:POOL_HALO, :] = jnp.zeros((POOL_HALO, POOL_WIDTH), jnp.float32)

    @pl.when(i > 0)
    def _():
        aext_ref[0:POOL_HALO, :] = aext_ref[tm:tm + POOL_HALO, :]

    aext_ref[POOL_HALO:, :] = proj(0)
    gate_a = _silu(proj(4))
    t_idx = i * tm + lax.broadcasted_iota(jnp.int32, (tm, 1), 0)
    for g, w in enumerate(POOL_WINDOWS):
        cols = slice(g * POOL_GROUP_DIM, (g + 1) * POOL_GROUP_DIM)
        e = aext_ref[:, cols]
        acc = e
        span = 1
        while span < w:
            acc = acc + pltpu.roll(acc, span, axis=0)
            span *= 2
        cnt = jnp.minimum(t_idx + 1, w).astype(jnp.float32)
        pooled = acc[POOL_HALO:, :] / cnt - e[POOL_HALO:, :]
        y = _dot(pooled.astype(jnp.bfloat16), pw_ref[g]) * ps_ref[:, cols]
        ya_ref[:, cols] = (y * gate_a[:, cols]).astype(ya_ref.dtype)


def _in0_call(x, pos_f, g, w_in, pool_w, pool_scale, invf, m1, m2):
    B, S, _ = x.shape
    tm = TM_IN0
    tok = lambda b, i: (b, i, 0)
    const2 = lambda b, i: (0, 0)
    out = jax.ShapeDtypeStruct((B, S, 1024), jnp.bfloat16)
    out_vt = jax.ShapeDtypeStruct((B, S // tm, ATTN_WIDTH, tm), jnp.bfloat16)
    spec = pl.BlockSpec((None, tm, 1024), tok)
    spec_vt = pl.BlockSpec((None, None, ATTN_WIDTH, tm), lambda b, i: (b, i, 0, 0))
    return pl.pallas_call(
        _in0_kernel,
        out_shape=[out, out, out_vt, out, out],
        grid=(B, S // tm),
        in_specs=[
            pl.BlockSpec((None, tm, D_MODEL), tok),
            pl.BlockSpec((None, tm, 1), tok),
            pl.BlockSpec((1, D_MODEL), const2),
            pl.BlockSpec((D_MODEL, IN_COLS), const2, pipeline_mode=pl.Buffered(1)),
            pl.BlockSpec((POOL_GROUPS, POOL_GROUP_DIM, POOL_GROUP_DIM), lambda b, i: (0, 0, 0)),
            pl.BlockSpec((1, POOL_WIDTH), const2),
            pl.BlockSpec((1, LANES), const2),
            pl.BlockSpec((1, LANES), const2),
            pl.BlockSpec((1, LANES), const2),
        ],
        out_specs=[spec, spec, spec_vt, spec, spec],
        scratch_shapes=[pltpu.VMEM((POOL_HALO + tm, POOL_WIDTH), jnp.float32)],
        compiler_params=pltpu.CompilerParams(
            dimension_semantics=("arbitrary", "arbitrary"),
            vmem_limit_bytes=VMEM_LIMIT),
        name="l0_in_proj",
    )(x, pos_f, g, w_in, pool_w, pool_scale, invf, m1, m2)


def _pair_tables(seq):
    nq = seq // TQ
    n_blocks = [(qi * TQ + TQ - 1) // TK + 1 for qi in range(nq)]
    full = [(qi, c) for qi in range(nq) for c in range(n_blocks[qi] - 1)]
    diag = [(qi, n_blocks[qi] - 1) for qi in range(nq)]
    pairs = np.array(full + diag, np.int32)
    return pairs[:, 0], pairs[:, 1], len(full)


def _attn_kernel(lam_init, n_full, n_pairs, qtab_ref, ctab_ref, zero_ref,
                 q_ref, k_ref, vt_ref, sgb_ref, lq1_ref, lk1_ref, lq2_ref, lk2_ref, sub_ref,
                 o_ref, qs_ref, vte_ref, m_ref, acc_ref, cm_ref, *s_refs):
    staged_rows = pl.ds(pl.multiple_of(zero_ref[0], TK), TK)
    nq = q_ref.shape[0] // TQ
    lane = lax.broadcasted_iota(jnp.int32, (TQ, LANES), 1)

    def stack_queries(i, carry):
        q = q_ref[pl.ds(pl.multiple_of(i * TQ, TQ), TQ), :]
        zero = jnp.zeros_like(q)
        qs_ref[i, 0:TQ, :] = jnp.where(lane < DIFF_QKDIM, q, zero)
        qs_ref[i, TQ:, :] = jnp.where(lane >= DIFF_QKDIM, q, zero)
        return carry

    lax.fori_loop(0, nq, stack_queries, 0)
    for c in range(vt_ref.shape[0]):
        vte_ref[c, 0:DIFF_VDIM, :] = vt_ref[c]
        vte_ref[c, DIFF_VDIM:, :] = jnp.ones((VT_PAD, TK), vte_ref.dtype)
    m_ref[...] = jnp.full(m_ref.shape, -jnp.inf, jnp.float32)
    acc_ref[...] = jnp.zeros(acc_ref.shape, jnp.float32)

    def scores(t, buf, masked):
        qi = qtab_ref[t]
        start = pl.multiple_of(ctab_ref[t] * TK, TK)
        kc = k_ref[pl.ds(start, TK), :]
        s = lax.dot_general(kc, qs_ref[qi], _NT, preferred_element_type=jnp.float32)
        if masked:
            key = lax.broadcasted_iota(jnp.int32, (TK, 2 * TQ), 0)
            col = lax.broadcasted_iota(jnp.int32, (TK, 2 * TQ), 1)
            qpos = qi * TQ + jnp.where(col >= TQ, col - TQ, col)
            s = jnp.where(start + key <= qpos, s, -jnp.inf)
        s_refs[buf][...] = s
        cm_ref[buf] = jnp.max(s, axis=0, keepdims=True)

    def softmax(t, buf):
        qi = qtab_ref[t]
        m_old = m_ref[qi]
        m_new = jnp.maximum(m_old, cm_ref[buf])
        m_ref[qi] = m_new
        s = s_refs[buf][staged_rows, :]
        p = jnp.exp2((s - m_new).astype(jnp.bfloat16))
        acc_ref[qi] = jnp.exp2(m_old - m_new) * acc_ref[qi] + _dot(vte_ref[ctab_ref[t]], p)

    def stage(t, phase, masked):
        scores(t, phase % PIPE_BUFS, masked)
        softmax(t - 1, (phase - 1) % PIPE_BUFS)

    def run_stages(first, count, masked):
        trips, rest = divmod(count, PIPE_UNROLL)

        def body(j, carry):
            for u in range(PIPE_UNROLL):
                stage(first + PIPE_UNROLL * j + u, first + u, masked)
            return carry

        lax.fori_loop(0, trips, body, 0)
        for u in range(rest):
            stage(first + PIPE_UNROLL * trips + u, first + u, masked)

    scores(0, 0, False)
    run_stages(1, n_full - 1, False)
    run_stages(n_full, n_pairs - n_full, True)
    softmax(n_pairs - 1, (n_pairs - 1) % PIPE_BUFS)

    lam = (jnp.exp(jnp.sum(lq1_ref[...] * lk1_ref[...], axis=-1, keepdims=True))
           - jnp.exp(jnp.sum(lq2_ref[...] * lk2_ref[...], axis=-1, keepdims=True))
           + lam_init)
    sub = sub_ref[...] * (1.0 - lam_init)

    def finish(i, carry):
        rows = pl.ds(pl.multiple_of(i * TQ, TQ), TQ)
        r = 1.0 / acc_ref[i, DIFF_VDIM:DIFF_VDIM + 1, :]
        acc = acc_ref[i, 0:DIFF_VDIM, :]
        ot = acc[:, 0:TQ] * r[:, 0:TQ] - lam * (acc[:, TQ:] * r[:, TQ:])
        ms = jnp.mean(ot * ot, axis=0, keepdims=True)
        ot = ot * lax.rsqrt(ms + SUBLN_EPS) * sub
        o_ref[rows, :] = (ot.T * sgb_ref[rows, :].astype(jnp.float32)).astype(o_ref.dtype)
        return carry

    lax.fori_loop(0, nq, finish, 0)


def _attn_call(q, k, vt, sgb, lq1, lk1, lq2, lk2, subln_col, lam_init):
    B, S, _ = q.shape
    nq = S // TQ
    qtab, ctab, n_full = _pair_tables(S)
    n_pairs = len(qtab)
    assert n_full >= 2
    zero = np.zeros((1,), np.int32)
    head = lambda b, h, *_: (b, 0, h)
    const2 = lambda b, h, *_: (0, 0)
    grid_spec = pltpu.PrefetchScalarGridSpec(
        num_scalar_prefetch=3,
        grid=(B, N_DIFF_HEADS),
        in_specs=[
            pl.BlockSpec((None, S, DIFF_VDIM), head),
            pl.BlockSpec((None, S, DIFF_VDIM), head),
            pl.BlockSpec((None, S // TK, DIFF_VDIM, TK), lambda b, h, *_: (b, 0, h, 0)),
            pl.BlockSpec((None, S, DIFF_VDIM), head),
            pl.BlockSpec((1, DIFF_QKDIM), const2),
            pl.BlockSpec((1, DIFF_QKDIM), const2),
            pl.BlockSpec((1, DIFF_QKDIM), const2),
            pl.BlockSpec((1, DIFF_QKDIM), const2),
            pl.BlockSpec((DIFF_VDIM, 1), const2),
        ],
        out_specs=pl.BlockSpec((None, S, DIFF_VDIM), head),
        scratch_shapes=[
            pltpu.VMEM((nq, 2 * TQ, DIFF_VDIM), jnp.bfloat16),
            pltpu.VMEM((S // TK, DIFF_VDIM + VT_PAD, TK), jnp.bfloat16),
            pltpu.VMEM((nq, 1, 2 * TQ), jnp.float32),
            pltpu.VMEM((nq, DIFF_VDIM + VT_PAD, 2 * TQ), jnp.float32),
            pltpu.VMEM((PIPE_BUFS, 1, 2 * TQ), jnp.float32),
        ] + [pltpu.VMEM((TK, 2 * TQ), jnp.float32)] * PIPE_BUFS,
    )
    return pl.pallas_call(
        functools.partial(_attn_kernel, lam_init, n_full, n_pairs),
        out_shape=jax.ShapeDtypeStruct((B, S, ATTN_WIDTH), jnp.bfloat16),
        grid_spec=grid_spec,
        compiler_params=pltpu.CompilerParams(
            dimension_semantics=("arbitrary", "arbitrary"),
            vmem_limit_bytes=VMEM_LIMIT),
        name="l0_diff_attn",
    )(qtab, ctab, zero, q, k, vt, sgb, lq1, lk1, lq2, lk2, subln_col)


def _out0_kernel(ya_ref, yb_ref, x_ref, w_ref, g_ref, o_ref):
    out = _dot(ya_ref[...], w_ref[0:POOL_WIDTH, :]) + _dot(yb_ref[...], w_ref[POOL_WIDTH:, :])
    o_ref[...] = x_ref[...] + _rms_norm(out, g_ref[...], EPS)


def _out0_call(ya, yb, x, w_out, g):
    n = x.shape[0]
    tm = TM_OUT0
    tok = lambda i: (i, 0)
    const2 = lambda i: (0, 0)
    return pl.pallas_call(
        _out0_kernel,
        out_shape=jax.ShapeDtypeStruct((n, D_MODEL), jnp.float32),
        grid=(n // tm,),
        in_specs=[
            pl.BlockSpec((tm, POOL_WIDTH), tok),
            pl.BlockSpec((tm, ATTN_WIDTH), tok),
            pl.BlockSpec((tm, D_MODEL), tok),
            pl.BlockSpec((D_INNER, D_MODEL), const2, pipeline_mode=pl.Buffered(1)),
            pl.BlockSpec((1, D_MODEL), const2),
        ],
        out_specs=pl.BlockSpec((tm, D_MODEL), tok),
        compiler_params=pltpu.CompilerParams(
            dimension_semantics=("arbitrary",),
            vmem_limit_bytes=VMEM_LIMIT),
        name="l0_out_proj",
    )(ya, yb, x, w_out, g)


def _l1_kernel(x_ref, g_ref, win_ref, lng_ref, lnb_ref, ws_ref, bs_ref, wout_ref, pg_ref,
               o_ref, u_ref, v_ref, y_ref):
    tm = x_ref.shape[0]
    x = x_ref[...]
    h = _rms_norm(x, g_ref[...], EPS).astype(jnp.bfloat16)
    nblk = D_INNER // 1024
    for j in range(nblk):
        cols = slice(j * 1024, (j + 1) * 1024)
        u_ref[:, cols] = _gelu_tanh(_dot(h, win_ref[:, j * 1024:(j + 1) * 1024]))
        v_ref[:, cols] = _gelu_tanh(_dot(h, win_ref[:, D_INNER + j * 1024:D_INNER + (j + 1) * 1024]))

    vf = v_ref[...]
    mu = jnp.mean(vf, axis=-1, keepdims=True)
    vc = vf - mu
    var = jnp.mean(vc * vc, axis=-1, keepdims=True)
    vn = (vc * lax.rsqrt(var + EPS) * lng_ref[...] + lnb_ref[...]).astype(jnp.bfloat16)

    tri_r = lax.broadcasted_iota(jnp.int32, (SGU_CHUNK, SGU_CHUNK), 0)
    tri_c = lax.broadcasted_iota(jnp.int32, (SGU_CHUNK, SGU_CHUNK), 1)
    tril = tri_c <= tri_r
    for g in range(SGU_GROUPS):
        cols = slice(g * SGU_GROUP_DIM, (g + 1) * SGU_GROUP_DIM)
        wg = jnp.where(tril, ws_ref[g], 0.0).astype(jnp.bfloat16)
        bias = bs_ref[:, g:g + 1]
        gate = _silu(_dot(h, win_ref[:, 2 * D_INNER + g * SGU_GROUP_DIM:
                                     2 * D_INNER + (g + 1) * SGU_GROUP_DIM]))
        for c in range(tm // SGU_CHUNK):
            rows = slice(c * SGU_CHUNK, (c + 1) * SGU_CHUNK)
            mixed = _dot(wg, vn[rows, cols]) + bias
            y_ref[rows, cols] = (u_ref[rows, cols] * mixed * gate[rows, :]).astype(y_ref.dtype)

    out = _dot(y_ref[...], wout_ref[...])
    o_ref[...] = x + _rms_norm(out, pg_ref[...], EPS)


def _l1_call(x, g, w_in, ln_g, ln_b, w_s, b_s_t, w_out, post_g):
    n = x.shape[0]
    tm = TM_L1
    tok = lambda i: (i, 0)
    const2 = lambda i: (0, 0)
    return pl.pallas_call(
        _l1_kernel,
        out_shape=jax.ShapeDtypeStruct((n, D_MODEL), jnp.float32),
        grid=(n // tm,),
        in_specs=[
            pl.BlockSpec((tm, D_MODEL), tok),
            pl.BlockSpec((1, D_MODEL), const2),
            pl.BlockSpec((D_MODEL, IN_COLS), const2, pipeline_mode=pl.Buffered(1)),
            pl.BlockSpec((1, D_INNER), const2),
            pl.BlockSpec((1, D_INNER), const2),
            pl.BlockSpec((SGU_GROUPS, SGU_CHUNK, SGU_CHUNK), lambda i: (0, 0, 0)),
            pl.BlockSpec((SGU_CHUNK, SGU_GROUPS), const2),
            pl.BlockSpec((D_INNER, D_MODEL), const2, pipeline_mode=pl.Buffered(1)),
            pl.BlockSpec((1, D_MODEL), const2),
        ],
        out_specs=pl.BlockSpec((tm, D_MODEL), tok),
        scratch_shapes=[
            pltpu.VMEM((tm, D_INNER), jnp.float32),
            pltpu.VMEM((tm, D_INNER), jnp.float32),
            pltpu.VMEM((tm, D_INNER), jnp.bfloat16),
        ],
        compiler_params=pltpu.CompilerParams(
            dimension_semantics=("arbitrary",),
            vmem_limit_bytes=VMEM_LIMIT),
        name="l1_sgu_layer",
    )(x, g, w_in, ln_g, ln_b, w_s, b_s_t, w_out, post_g)


def _rope_lane_tables():
    half = ROPE_DIM // 2
    inv_freq = jnp.power(ROPE_THETA, -jnp.arange(half, dtype=jnp.float32) * 2.0 / ROPE_DIM)
    d = jnp.arange(LANES) % DIFF_QKDIM
    invf = jnp.where(d < ROPE_DIM, inv_freq[d % half], 0.0).astype(jnp.float32)
    m1 = jnp.where(d < half, -1.0, 0.0).astype(jnp.float32)
    m2 = jnp.where((d >= half) & (d < ROPE_DIM), 1.0, 0.0).astype(jnp.float32)
    return invf[None, :], m1[None, :], m2[None, :]


def kernel(x, positions, pre_norm, post_norm, w_in, w_out, pool_w, pool_scale, lam_q1, lam_k1, lam_q2, lam_k2, diff_subln, sgu_ln_g, sgu_ln_b, sgu_w, sgu_b):
    B, S, D = x.shape
    bf16 = jnp.bfloat16
    w_in_b = w_in.astype(bf16)
    w_out_b = w_out.astype(bf16)
    invf, m1, m2 = _rope_lane_tables()
    pos_f = positions.astype(jnp.float32)[:, :, None]

    q, k, vt, ya, sgb = _in0_call(x, pos_f, pre_norm[0:1], w_in_b[0], pool_w[0].astype(bf16),
                                  pool_scale[0:1], invf, m1, m2)
    lam_init = 0.8 - 0.6 * math.exp(-0.3 * 0)
    yb = _attn_call(q, k, vt, sgb, lam_q1[0:1], lam_k1[0:1], lam_q2[0:1], lam_k2[0:1],
                    diff_subln[0].reshape(DIFF_VDIM, 1), lam_init)
    n = B * S
    x1 = _out0_call(ya.reshape(n, POOL_WIDTH), yb.reshape(n, ATTN_WIDTH), x.reshape(n, D),
                    w_out_b[0], post_norm[0:1])

    x2 = _l1_call(x1, pre_norm[1:2], w_in_b[1], sgu_ln_g[0:1], sgu_ln_b[0:1], sgu_w[0],
                  sgu_b[0].T, w_out_b[1], post_norm[1:2])
    return x2.reshape(B, S, D)
```

```python
import functools
import math

import jax
import jax.numpy as jnp
import numpy as np
from jax import lax
from jax.experimental import pallas as pl
from jax.experimental.pallas import tpu as pltpu

D_MODEL = 1024
D_INNER = 2048
IN_COLS = 3 * D_INNER
POOL_WIDTH = 1024
POOL_GROUPS = 4
POOL_GROUP_DIM = 256
POOL_WINDOWS = (2, 4, 8, 16)
POOL_HALO = 16
ATTN_WIDTH = 1024
N_DIFF_HEADS = 8
DIFF_VDIM = 128
DIFF_QKDIM = 64
ROPE_THETA = 500000.0
ROPE_DIM = 16
SGU_CHUNK = 128
SGU_GROUPS = 8
SGU_GROUP_DIM = 256
EPS = 1e-6
SUBLN_EPS = 1e-5
LANES = 128
LOG2_E = math.log2(math.e)

TQ = 256
TK = 512
TM_IN0 = TK
VT_PAD = 16
PIPE_BUFS = 3
PIPE_UNROLL = 12
assert PIPE_UNROLL % PIPE_BUFS == 0
TM_OUT0 = 512
TM_L1 = 256
VMEM_LIMIT = 56 * 1024 * 1024

_NT = (((1,), (1,)), ((), ()))


def _rms_norm(x, g, eps):
    return x * lax.rsqrt(jnp.mean(x * x, axis=-1, keepdims=True) + eps) * g


def _silu(x):
    return x / (1.0 + jnp.exp(-x))


def _gelu_tanh(x):
    c = math.sqrt(2.0 / math.pi)
    return 0.5 * x * (1.0 + jnp.tanh(c * (x + 0.044715 * (x * x * x))))


def _dot(a, b):
    return jnp.dot(a, b, preferred_element_type=jnp.float32)


def _in0_kernel(x_ref, pos_ref, g_ref, w_ref, pw_ref, ps_ref, invf_ref, m1_ref, m2_ref,
                q_ref, k_ref, vt_ref, ya_ref, sgb_ref, aext_ref):
    i = pl.program_id(1)
    tm = x_ref.shape[0]
    h = _rms_norm(x_ref[...], g_ref[...], EPS).astype(jnp.bfloat16)

    def proj(col):
        return _dot(h, w_ref[:, col * 1024:(col + 1) * 1024])

    ang = pos_ref[...] * invf_ref[...]
    cs = jnp.cos(ang)
    sn = jnp.sin(ang)
    s_lo = sn * m1_ref[...]
    s_hi = sn * m2_ref[...]
    half = ROPE_DIM // 2

    def rope_store(t, out_ref, scale):
        for c in range(ATTN_WIDTH // LANES):
            tc = t[:, c * LANES:(c + 1) * LANES]
            up = pltpu.roll(tc, LANES - half, axis=1)
            dn = pltpu.roll(tc, half, axis=1)
            r = tc * cs + up * s_lo + dn * s_hi
            if scale != 1.0:
                r = r * scale
            out_ref[:, c * LANES:(c + 1) * LANES] = r.astype(out_ref.dtype)

    rope_store(proj(1), q_ref, DIFF_QKDIM ** -0.5 * LOG2_E)
    rope_store(proj(2), k_ref, 1.0)
    vt_ref[...] = proj(3).T.astype(vt_ref.dtype)
    sgb_ref[...] = _silu(proj(5)).astype(sgb_ref.dtype)

    @pl.when(i == 0)
    def _():
        aext_ref[0:POOL_HALO, :] = jnp.zeros((POOL_HALO, POOL_WIDTH), jnp.float32)

    @pl.when(i > 0)
    def _():
        aext_ref[0:POOL_HALO, :] = aext_ref[tm:tm + POOL_HALO, :]

    aext_ref[POOL_HALO:, :] = proj(0)
    gate_a = _silu(proj(4))
    t_idx = i * tm + lax.broadcasted_iota(jnp.int32, (tm, 1), 0)
    for g, w in enumerate(POOL_WINDOWS):
        cols = slice(g * POOL_GROUP_DIM, (g + 1) * POOL_GROUP_DIM)
        e = aext_ref[:, cols]
        acc = e
        span = 1
        while span < w:
            acc = acc + pltpu.roll(acc, span, axis=0)
            span *= 2
        cnt = jnp.minimum(t_idx + 1, w).astype(jnp.float32)
        pooled = acc[POOL_HALO:, :] / cnt - e[POOL_HALO:, :]
        y = _dot(pooled.astype(jnp.bfloat16), pw_ref[g]) * ps_ref[:, cols]
        ya_ref[:, cols] = (y * gate_a[:, cols]).astype(ya_ref.dtype)


def _in0_call(x, pos_f, g, w_in, pool_w, pool_scale, invf, m1, m2):
    B, S, _ = x.shape
    tm = TM_IN0
    tok = lambda b, i: (b, i, 0)
    const2 = lambda b, i: (0, 0)
    out = jax.ShapeDtypeStruct((B, S, 1024), jnp.bfloat16)
    out_vt = jax.ShapeDtypeStruct((B, S // tm, ATTN_WIDTH, tm), jnp.bfloat16)
    spec = pl.BlockSpec((None, tm, 1024), tok)
    spec_vt = pl.BlockSpec((None, None, ATTN_WIDTH, tm), lambda b, i: (b, i, 0, 0))
    return pl.pallas_call(
        _in0_kernel,
        out_shape=[out, out, out_vt, out, out],
        grid=(B, S // tm),
        in_specs=[
            pl.BlockSpec((None, tm, D_MODEL), tok),
            pl.BlockSpec((None, tm, 1), tok),
            pl.BlockSpec((1, D_MODEL), const2),
            pl.BlockSpec((D_MODEL, IN_COLS), const2, pipeline_mode=pl.Buffered(1)),
            pl.BlockSpec((POOL_GROUPS, POOL_GROUP_DIM, POOL_GROUP_DIM), lambda b, i: (0, 0, 0)),
            pl.BlockSpec((1, POOL_WIDTH), const2),
            pl.BlockSpec((1, LANES), const2),
            pl.BlockSpec((1, LANES), const2),
            pl.BlockSpec((1, LANES), const2),
        ],
        out_specs=[spec, spec, spec_vt, spec, spec],
        scratch_shapes=[pltpu.VMEM((POOL_HALO + tm, POOL_WIDTH), jnp.float32)],
        compiler_params=pltpu.CompilerParams(
            dimension_semantics=("arbitrary", "arbitrary"),
            vmem_limit_bytes=VMEM_LIMIT),
        name="l0_in_proj",
    )(x, pos_f, g, w_in, pool_w, pool_scale, invf, m1, m2)


def _pair_tables(seq):
    nq = seq // TQ
    n_blocks = [(qi * TQ + TQ - 1) // TK + 1 for qi in range(nq)]
    full = [(qi, c) for qi in range(nq) for c in range(n_blocks[qi] - 1)]
    diag = [(qi, n_blocks[qi] - 1) for qi in range(nq)]
    pairs = np.array(full + diag, np.int32)
    return pairs[:, 0], pairs[:, 1], len(full)


def _attn_kernel(lam_init, n_full, n_pairs, qtab_ref, ctab_ref, zero_ref,
                 q_ref, k_ref, vt_ref, sgb_ref, lq1_ref, lk1_ref, lq2_ref, lk2_ref, sub_ref,
                 o_ref, qs_ref, vte_ref, m_ref, acc_ref, cm_ref, *s_refs):
    staged_rows = pl.ds(pl.multiple_of(zero_ref[0], TK), TK)
    nq = q_ref.shape[0] // TQ
    lane = lax.broadcasted_iota(jnp.int32, (TQ, LANES), 1)

    def stack_queries(i, carry):
        q = q_ref[pl.ds(pl.multiple_of(i * TQ, TQ), TQ), :]
        zero = jnp.zeros_like(q)
        qs_ref[i, 0:TQ, :] = jnp.where(lane < DIFF_QKDIM, q, zero)
        qs_ref[i, TQ:, :] = jnp.where(lane >= DIFF_QKDIM, q, zero)
        return carry

    lax.fori_loop(0, nq, stack_queries, 0)
    for c in range(vt_ref.shape[0]):
        vte_ref[c, 0:DIFF_VDIM, :] = vt_ref[c]
        vte_ref[c, DIFF_VDIM:, :] = jnp.ones((VT_PAD, TK), vte_ref.dtype)
    m_ref[...] = jnp.full(m_ref.shape, -jnp.inf, jnp.float32)
    acc_ref[...] = jnp.zeros(acc_ref.shape, jnp.float32)

    def scores(t, buf, masked):
        qi = qtab_ref[t]
        start = pl.multiple_of(ctab_ref[t] * TK, TK)
        kc = k_ref[pl.ds(start, TK), :]
        s = lax.dot_general(kc, qs_ref[qi], _NT, preferred_element_type=jnp.float32)
        if masked:
            key = lax.broadcasted_iota(jnp.int32, (TK, 2 * TQ), 0)
            col = lax.broadcasted_iota(jnp.int32, (TK, 2 * TQ), 1)
            qpos = qi * TQ + jnp.where(col >= TQ, col - TQ, col)
            s = jnp.where(start + key <= qpos, s, -jnp.inf)
        s_refs[buf][...] = s
        cm_ref[buf] = jnp.max(s, axis=0, keepdims=True)

    def softmax(t, buf):
        qi = qtab_ref[t]
        m_old = m_ref[qi]
        m_new = jnp.maximum(m_old, cm_ref[buf])
        m_ref[qi] = m_new
        s = s_refs[buf][staged_rows, :]
        p = jnp.exp2((s - m_new).astype(jnp.bfloat16))
        acc_ref[qi] = jnp.exp2(m_old - m_new) * acc_ref[qi] + _dot(vte_ref[ctab_ref[t]], p)

    def stage(t, phase, masked):
        scores(t, phase % PIPE_BUFS, masked)
        softmax(t - 1, (phase - 1) % PIPE_BUFS)

    def run_stages(first, count, masked):
        trips, rest = divmod(count, PIPE_UNROLL)

        def body(j, carry):
            for u in range(PIPE_UNROLL):
                stage(first + PIPE_UNROLL * j + u, first + u, masked)
            return carry

        lax.fori_loop(0, trips, body, 0)
        for u in range(rest):
            stage(first + PIPE_UNROLL * trips + u, first + u, masked)

    scores(0, 0, False)
    run_stages(1, n_full - 1, False)
    run_stages(n_full, n_pairs - n_full, True)
    softmax(n_pairs - 1, (n_pairs - 1) % PIPE_BUFS)

    lam = (jnp.exp(jnp.sum(lq1_ref[...] * lk1_ref[...], axis=-1, keepdims=True))
           - jnp.exp(jnp.sum(lq2_ref[...] * lk2_ref[...], axis=-1, keepdims=True))
           + lam_init)
    sub = sub_ref[...] * (1.0 - lam_init)

    def finish(i, carry):
        rows = pl.ds(pl.multiple_of(i * TQ, TQ), TQ)
        r = 1.0 / acc_ref[i, DIFF_VDIM:DIFF_VDIM + 1, :]
        acc = acc_ref[i, 0:DIFF_VDIM, :]
        ot = acc[:, 0:TQ] * r[:, 0:TQ] - lam * (acc[:, TQ:] * r[:, TQ:])
        ms = jnp.mean(ot * ot, axis=0, keepdims=True)
        ot = ot * lax.rsqrt(ms + SUBLN_EPS) * sub
        o_ref[rows, :] = (ot.T * sgb_ref[rows, :].astype(jnp.float32)).astype(o_ref.dtype)
        return carry

    lax.fori_loop(0, nq, finish, 0, unroll=2)


def _attn_call(q, k, vt, sgb, lq1, lk1, lq2, lk2, subln_col, lam_init):
    B, S, _ = q.shape
    nq = S // TQ
    qtab, ctab, n_full = _pair_tables(S)
    n_pairs = len(qtab)
    assert n_full >= 2
    zero = np.zeros((1,), np.int32)
    head = lambda b, h, *_: (b, 0, h)
    const2 = lambda b, h, *_: (0, 0)
    grid_spec = pltpu.PrefetchScalarGridSpec(
        num_scalar_prefetch=3,
        grid=(B, N_DIFF_HEADS),
        in_specs=[
            pl.BlockSpec((None, S, DIFF_VDIM), head),
            pl.BlockSpec((None, S, DIFF_VDIM), head),
            pl.BlockSpec((None, S // TK, DIFF_VDIM, TK), lambda b, h, *_: (b, 0, h, 0)),
            pl.BlockSpec((None, S, DIFF_VDIM), head),
            pl.BlockSpec((1, DIFF_QKDIM), const2),
            pl.BlockSpec((1, DIFF_QKDIM), const2),
            pl.BlockSpec((1, DIFF_QKDIM), const2),
            pl.BlockSpec((1, DIFF_QKDIM), const2),
            pl.BlockSpec((DIFF_VDIM, 1), const2),
        ],
        out_specs=pl.BlockSpec((None, S, DIFF_VDIM), head),
        scratch_shapes=[
            pltpu.VMEM((nq, 2 * TQ, DIFF_VDIM), jnp.bfloat16),
            pltpu.VMEM((S // TK, DIFF_VDIM + VT_PAD, TK), jnp.bfloat16),
            pltpu.VMEM((nq, 1, 2 * TQ), jnp.float32),
            pltpu.VMEM((nq, DIFF_VDIM + VT_PAD, 2 * TQ), jnp.float32),
            pltpu.VMEM((PIPE_BUFS, 1, 2 * TQ), jnp.float32),
        ] + [pltpu.VMEM((TK, 2 * TQ), jnp.float32)] * PIPE_BUFS,
    )
    return pl.pallas_call(
        functools.partial(_attn_kernel, lam_init, n_full, n_pairs),
        out_shape=jax.ShapeDtypeStruct((B, S, ATTN_WIDTH), jnp.bfloat16),
        grid_spec=grid_spec,
        compiler_params=pltpu.CompilerParams(
            dimension_semantics=("arbitrary", "arbitrary"),
            vmem_limit_bytes=VMEM_LIMIT),
        name="l0_diff_attn",
    )(qtab, ctab, zero, q, k, vt, sgb, lq1, lk1, lq2, lk2, subln_col)


def _out0_kernel(ya_ref, yb_ref, x_ref, w_ref, g_ref, o_ref):
    out = _dot(ya_ref[...], w_ref[0:POOL_WIDTH, :]) + _dot(yb_ref[...], w_ref[POOL_WIDTH:, :])
    o_ref[...] = x_ref[...] + _rms_norm(out, g_ref[...], EPS)


def _out0_call(ya, yb, x, w_out, g):
    n = x.shape[0]
    tm = TM_OUT0
    tok = lambda i: (i, 0)
    const2 = lambda i: (0, 0)
    return pl.pallas_call(
        _out0_kernel,
        out_shape=jax.ShapeDtypeStruct((n, D_MODEL), jnp.float32),
        grid=(n // tm,),
        in_specs=[
            pl.BlockSpec((tm, POOL_WIDTH), tok),
            pl.BlockSpec((tm, ATTN_WIDTH), tok),
            pl.BlockSpec((tm, D_MODEL), tok),
            pl.BlockSpec((D_INNER, D_MODEL), const2, pipeline_mode=pl.Buffered(1)),
            pl.BlockSpec((1, D_MODEL), const2),
        ],
        out_specs=pl.BlockSpec((tm, D_MODEL), tok),
        compiler_params=pltpu.CompilerParams(
            dimension_semantics=("arbitrary",),
            vmem_limit_bytes=VMEM_LIMIT),
        name="l0_out_proj",
    )(ya, yb, x, w_out, g)


def _l1_kernel(x_ref, g_ref, win_ref, lng_ref, lnb_ref, ws_ref, bs_ref, wout_ref, pg_ref,
               o_ref, u_ref, v_ref, y_ref):
    tm = x_ref.shape[0]
    x = x_ref[...]
    h = _rms_norm(x, g_ref[...], EPS).astype(jnp.bfloat16)
    nblk = D_INNER // 1024
    for j in range(nblk):
        cols = slice(j * 1024, (j + 1) * 1024)
        u_ref[:, cols] = _gelu_tanh(_dot(h, win_ref[:, j * 1024:(j + 1) * 1024]))
        v_ref[:, cols] = _gelu_tanh(_dot(h, win_ref[:, D_INNER + j * 1024:D_INNER + (j + 1) * 1024]))

    vf = v_ref[...]
    mu = jnp.mean(vf, axis=-1, keepdims=True)
    vc = vf - mu
    var = jnp.mean(vc * vc, axis=-1, keepdims=True)
    vn = (vc * lax.rsqrt(var + EPS) * lng_ref[...] + lnb_ref[...]).astype(jnp.bfloat16)

    tri_r = lax.broadcasted_iota(jnp.int32, (SGU_CHUNK, SGU_CHUNK), 0)
    tri_c = lax.broadcasted_iota(jnp.int32, (SGU_CHUNK, SGU_CHUNK), 1)
    tril = tri_c <= tri_r
    for g in range(SGU_GROUPS):
        cols = slice(g * SGU_GROUP_DIM, (g + 1) * SGU_GROUP_DIM)
        wg = jnp.where(tril, ws_ref[g], 0.0).astype(jnp.bfloat16)
        bias = bs_ref[:, g:g + 1]
        gate = _silu(_dot(h, win_ref[:, 2 * D_INNER + g * SGU_GROUP_DIM:
                                     2 * D_INNER + (g + 1) * SGU_GROUP_DIM]))
        for c in range(tm // SGU_CHUNK):
            rows = slice(c * SGU_CHUNK, (c + 1) * SGU_CHUNK)
            mixed = _dot(wg, vn[rows, cols]) + bias
            y_ref[rows, cols] = (u_ref[rows, cols] * mixed * gate[rows, :]).astype(y_ref.dtype)

    out = _dot(y_ref[...], wout_ref[...])
    o_ref[...] = x + _rms_norm(out, pg_ref[...], EPS)


def _l1_call(x, g, w_in, ln_g, ln_b, w_s, b_s_t, w_out, post_g):
    n = x.shape[0]
    tm = TM_L1
    tok = lambda i: (i, 0)
    const2 = lambda i: (0, 0)
    return pl.pallas_call(
        _l1_kernel,
        out_shape=jax.ShapeDtypeStruct((n, D_MODEL), jnp.float32),
        grid=(n // tm,),
        in_specs=[
            pl.BlockSpec((tm, D_MODEL), tok),
            pl.BlockSpec((1, D_MODEL), const2),
            pl.BlockSpec((D_MODEL, IN_COLS), const2, pipeline_mode=pl.Buffered(1)),
            pl.BlockSpec((1, D_INNER), const2),
            pl.BlockSpec((1, D_INNER), const2),
            pl.BlockSpec((SGU_GROUPS, SGU_CHUNK, SGU_CHUNK), lambda i: (0, 0, 0)),
            pl.BlockSpec((SGU_CHUNK, SGU_GROUPS), const2),
            pl.BlockSpec((D_INNER, D_MODEL), const2, pipeline_mode=pl.Buffered(1)),
            pl.BlockSpec((1, D_MODEL), const2),
        ],
        out_specs=pl.BlockSpec((tm, D_MODEL), tok),
        scratch_shapes=[
            pltpu.VMEM((tm, D_INNER), jnp.float32),
            pltpu.VMEM((tm, D_INNER), jnp.float32),
            pltpu.VMEM((tm, D_INNER), jnp.bfloat16),
        ],
        compiler_params=pltpu.CompilerParams(
            dimension_semantics=("arbitrary",),
            vmem_limit_bytes=VMEM_LIMIT),
        name="l1_sgu_layer",
    )(x, g, w_in, ln_g, ln_b, w_s, b_s_t, w_out, post_g)


def _rope_lane_tables():
    half = ROPE_DIM // 2
    inv_freq = jnp.power(ROPE_THETA, -jnp.arange(half, dtype=jnp.float32) * 2.0 / ROPE_DIM)
    d = jnp.arange(LANES) % DIFF_QKDIM
    invf = jnp.where(d < ROPE_DIM, inv_freq[d % half], 0.0).astype(jnp.float32)
    m1 = jnp.where(d < half, -1.0, 0.0).astype(jnp.float32)
    m2 = jnp.where((d >= half) & (d < ROPE_DIM), 1.0, 0.0).astype(jnp.float32)
    return invf[None, :], m1[None, :], m2[None, :]


def kernel(x, positions, pre_norm, post_norm, w_in, w_out, pool_w, pool_scale, lam_q1, lam_k1, lam_q2, lam_k2, diff_subln, sgu_ln_g, sgu_ln_b, sgu_w, sgu_b):
    B, S, D = x.shape
    bf16 = jnp.bfloat16
    w_in_b = [w_in[i].astype(bf16) for i in range(w_in.shape[0])]
    w_out_b = [w_out[i].astype(bf16) for i in range(w_out.shape[0])]
    invf, m1, m2 = _rope_lane_tables()
    pos_f = positions.astype(jnp.float32)[:, :, None]

    q, k, vt, ya, sgb = _in0_call(x, pos_f, pre_norm[0:1], w_in_b[0], pool_w[0].astype(bf16),
                                  pool_scale[0:1], invf, m1, m2)
    lam_init = 0.8 - 0.6 * math.exp(-0.3 * 0)
    yb = _attn_call(q, k, vt, sgb, lam_q1[0:1], lam_k1[0:1], lam_q2[0:1], lam_k2[0:1],
                    diff_subln[0].reshape(DIFF_VDIM, 1), lam_init)
    n = B * S
    x1 = _out0_call(ya.reshape(n, POOL_WIDTH), yb.reshape(n, ATTN_WIDTH), x.reshape(n, D),
                    w_out_b[0], post_norm[0:1])

    x2 = _l1_call(x1, pre_norm[1:2], w_in_b[1], sgu_ln_g[0:1], sgu_ln_b[0:1], sgu_w[0],
                  sgu_b[0].T, w_out_b[1], post_norm[1:2])
    return x2.reshape(B, S, D)
```

```python
import functools
import math

import jax
import jax.numpy as jnp
import numpy as np
from jax import lax
from jax.experimental import pallas as pl
from jax.experimental.pallas import tpu as pltpu

D_MODEL = 1024
D_INNER = 2048
IN_COLS = 3 * D_INNER
POOL_WIDTH = 1024
POOL_GROUPS = 4
POOL_GROUP_DIM = 256
POOL_WINDOWS = (2, 4, 8, 16)
POOL_HALO = 16
ATTN_WIDTH = 1024
N_DIFF_HEADS = 8
DIFF_VDIM = 128
DIFF_QKDIM = 64
ROPE_THETA = 500000.0
ROPE_DIM = 16
SGU_CHUNK = 128
SGU_GROUPS = 8
SGU_GROUP_DIM = 256
EPS = 1e-6
SUBLN_EPS = 1e-5
LANES = 128
LOG2_E = math.log2(math.e)

TQ = 256
TK = 512
TM_IN0 = TK
VT_PAD = 16
PIPE_BUFS = 3
PIPE_UNROLL = 12
assert PIPE_UNROLL % PIPE_BUFS == 0
TM_OUT0 = 512
TM_L1 = 512
L1_COL_BLOCK = 512
L1_RAW_BUFS = 3
VMEM_LIMIT = 56 * 1024 * 1024

_NT = (((1,), (1,)), ((), ()))


def _rms_norm(x, g, eps):
    return x * lax.rsqrt(jnp.mean(x * x, axis=-1, keepdims=True) + eps) * g


def _silu(x):
    return x / (1.0 + jnp.exp(-x))


def _gelu_tanh(x):
    c = math.sqrt(2.0 / math.pi)
    return 0.5 * x * (1.0 + jnp.tanh(c * (x + 0.044715 * (x * x * x))))


def _dot(a, b):
    return jnp.dot(a, b, preferred_element_type=jnp.float32)


def _in0_kernel(x_ref, pos_ref, g_ref, w_ref, pw_ref, ps_ref, invf_ref, m1_ref, m2_ref,
                q_ref, k_ref, vt_ref, ya_ref, sgb_ref, aext_ref):
    i = pl.program_id(1)
    tm = x_ref.shape[0]
    h = _rms_norm(x_ref[...], g_ref[...], EPS).astype(jnp.bfloat16)

    def proj(col):
        return _dot(h, w_ref[:, col * 1024:(col + 1) * 1024])

    ang = pos_ref[...] * invf_ref[...]
    cs = jnp.cos(ang)
    sn = jnp.sin(ang)
    s_lo = sn * m1_ref[...]
    s_hi = sn * m2_ref[...]
    half = ROPE_DIM // 2

    def rope_store(t, out_ref, scale):
        for c in range(ATTN_WIDTH // LANES):
            tc = t[:, c * LANES:(c + 1) * LANES]
            up = pltpu.roll(tc, LANES - half, axis=1)
            dn = pltpu.roll(tc, half, axis=1)
            r = tc * cs + up * s_lo + dn * s_hi
            if scale != 1.0:
                r = r * scale
            out_ref[:, c * LANES:(c + 1) * LANES] = r.astype(out_ref.dtype)

    rope_store(proj(1), q_ref, DIFF_QKDIM ** -0.5 * LOG2_E)
    rope_store(proj(2), k_ref, 1.0)
    vt_ref[...] = proj(3).T.astype(vt_ref.dtype)
    sgb_ref[...] = _silu(proj(5)).astype(sgb_ref.dtype)

    @pl.when(i == 0)
    def _():
        aext_ref[0:POOL_HALO, :] = jnp.zeros((POOL_HALO, POOL_WIDTH), jnp.float32)

    @pl.when(i > 0)
    def _():
        aext_ref[0:POOL_HALO, :] = aext_ref[tm:tm + POOL_HALO, :]

    aext_ref[POOL_HALO:, :] = proj(0)
    gate_a = _silu(proj(4))
    t_idx = i * tm + lax.broadcasted_iota(jnp.int32, (tm, 1), 0)
    for g, w in enumerate(POOL_WINDOWS):
        cols = slice(g * POOL_GROUP_DIM, (g + 1) * POOL_GROUP_DIM)
        e = aext_ref[:, cols]
        acc = e
        span = 1
        while span < w:
            acc = acc + pltpu.roll(acc, span, axis=0)
            span *= 2
        cnt = jnp.minimum(t_idx + 1, w).astype(jnp.float32)
        pooled = acc[POOL_HALO:, :] / cnt - e[POOL_HALO:, :]
        y = _dot(pooled.astype(jnp.bfloat16), pw_ref[g]) * ps_ref[:, cols]
        ya_ref[:, cols] = (y * gate_a[:, cols]).astype(ya_ref.dtype)


def _in0_call(x, pos_f, g, w_in, pool_w, pool_scale, invf, m1, m2, layer):
    B, S, _ = x.shape
    tm = TM_IN0
    tok = lambda b, i: (b, i, 0)
    const2 = lambda b, i: (0, 0)
    out = jax.ShapeDtypeStruct((B, S, 1024), jnp.bfloat16)
    out_vt = jax.ShapeDtypeStruct((B, S // tm, ATTN_WIDTH, tm), jnp.bfloat16)
    spec = pl.BlockSpec((None, tm, 1024), tok)
    spec_vt = pl.BlockSpec((None, None, ATTN_WIDTH, tm), lambda b, i: (b, i, 0, 0))
    return pl.pallas_call(
        _in0_kernel,
        out_shape=[out, out, out_vt, out, out],
        grid=(B, S // tm),
        in_specs=[
            pl.BlockSpec((None, tm, D_MODEL), tok),
            pl.BlockSpec((None, tm, 1), tok),
            pl.BlockSpec((1, D_MODEL), const2),
            pl.BlockSpec((None, D_MODEL, IN_COLS), lambda b, i: (layer, 0, 0),
                         pipeline_mode=pl.Buffered(1)),
            pl.BlockSpec((POOL_GROUPS, POOL_GROUP_DIM, POOL_GROUP_DIM), lambda b, i: (0, 0, 0)),
            pl.BlockSpec((1, POOL_WIDTH), const2),
            pl.BlockSpec((1, LANES), const2),
            pl.BlockSpec((1, LANES), const2),
            pl.BlockSpec((1, LANES), const2),
        ],
        out_specs=[spec, spec, spec_vt, spec, spec],
        scratch_shapes=[pltpu.VMEM((POOL_HALO + tm, POOL_WIDTH), jnp.float32)],
        compiler_params=pltpu.CompilerParams(
            dimension_semantics=("arbitrary", "arbitrary"),
            vmem_limit_bytes=VMEM_LIMIT),
        name="l0_in_proj",
    )(x, pos_f, g, w_in, pool_w, pool_scale, invf, m1, m2)


def _pair_tables(seq):
    nq = seq // TQ
    n_blocks = [(qi * TQ + TQ - 1) // TK + 1 for qi in range(nq)]
    full = [(qi, c) for qi in range(nq) for c in range(n_blocks[qi] - 1)]
    diag = [(qi, n_blocks[qi] - 1) for qi in range(nq)]
    pairs = np.array(full + diag, np.int32)
    return pairs[:, 0], pairs[:, 1], len(full)


def _attn_kernel(lam_init, n_full, n_pairs, qtab_ref, ctab_ref, zero_ref,
                 q_ref, k_ref, vt_ref, sgb_ref, lq1_ref, lk1_ref, lq2_ref, lk2_ref, sub_ref,
                 o_ref, qs_ref, vte_ref, m_ref, acc_ref, cm_ref, *s_refs):
    staged_rows = pl.ds(pl.multiple_of(zero_ref[0], TK), TK)
    nq = q_ref.shape[0] // TQ
    lane = lax.broadcasted_iota(jnp.int32, (TQ, LANES), 1)

    def stack_queries(i, carry):
        q = q_ref[pl.ds(pl.multiple_of(i * TQ, TQ), TQ), :]
        zero = jnp.zeros_like(q)
        qs_ref[i, 0:TQ, :] = jnp.where(lane < DIFF_QKDIM, q, zero)
        qs_ref[i, TQ:, :] = jnp.where(lane >= DIFF_QKDIM, q, zero)
        return carry

    lax.fori_loop(0, nq, stack_queries, 0)
    for c in range(vt_ref.shape[0]):
        vte_ref[c, 0:DIFF_VDIM, :] = vt_ref[c]
        vte_ref[c, DIFF_VDIM:, :] = jnp.ones((VT_PAD, TK), vte_ref.dtype)
    m_ref[...] = jnp.full(m_ref.shape, -jnp.inf, jnp.float32)
    acc_ref[...] = jnp.zeros(acc_ref.shape, jnp.float32)

    def scores(t, buf, masked):
        qi = qtab_ref[t]
        start = pl.multiple_of(ctab_ref[t] * TK, TK)
        kc = k_ref[pl.ds(start, TK), :]
        s = lax.dot_general(kc, qs_ref[qi], _NT, preferred_element_type=jnp.float32)
        if masked:
            key = lax.broadcasted_iota(jnp.int32, (TK, 2 * TQ), 0)
            col = lax.broadcasted_iota(jnp.int32, (TK, 2 * TQ), 1)
            qpos = qi * TQ + jnp.where(col >= TQ, col - TQ, col)
            s = jnp.where(start + key <= qpos, s, -jnp.inf)
        s_refs[buf][...] = s
        cm_ref[buf] = jnp.max(s, axis=0, keepdims=True)

    def softmax(t, buf):
        qi = qtab_ref[t]
        m_old = m_ref[qi]
        m_new = jnp.maximum(m_old, cm_ref[buf])
        m_ref[qi] = m_new
        s = s_refs[buf][staged_rows, :]
        p = jnp.exp2((s - m_new).astype(jnp.bfloat16))
        acc_ref[qi] = jnp.exp2(m_old - m_new) * acc_ref[qi] + _dot(vte_ref[ctab_ref[t]], p)

    def stage(t, phase, masked):
        scores(t, phase % PIPE_BUFS, masked)
        softmax(t - 1, (phase - 1) % PIPE_BUFS)

    def run_stages(first, count, masked):
        trips, rest = divmod(count, PIPE_UNROLL)

        def body(j, carry):
            for u in range(PIPE_UNROLL):
                stage(first + PIPE_UNROLL * j + u, first + u, masked)
            return carry

        lax.fori_loop(0, trips, body, 0)
        for u in range(rest):
            stage(first + PIPE_UNROLL * trips + u, first + u, masked)

    scores(0, 0, False)
    run_stages(1, n_full - 1, False)
    run_stages(n_full, n_pairs - n_full, True)
    softmax(n_pairs - 1, (n_pairs - 1) % PIPE_BUFS)

    lam = (jnp.exp(jnp.sum(lq1_ref[...] * lk1_ref[...], axis=-1, keepdims=True))
           - jnp.exp(jnp.sum(lq2_ref[...] * lk2_ref[...], axis=-1, keepdims=True))
           + lam_init)
    sub = sub_ref[...] * (1.0 - lam_init)

    def finish(i, carry):
        rows = pl.ds(pl.multiple_of(i * TQ, TQ), TQ)
        r = 1.0 / acc_ref[i, DIFF_VDIM:DIFF_VDIM + 1, :]
        acc = acc_ref[i, 0:DIFF_VDIM, :]
        ot = acc[:, 0:TQ] * r[:, 0:TQ] - lam * (acc[:, TQ:] * r[:, TQ:])
        ms = jnp.mean(ot * ot, axis=0, keepdims=True)
        ot = ot * lax.rsqrt(ms + SUBLN_EPS) * sub
        o_ref[rows, :] = (ot.T * sgb_ref[rows, :].astype(jnp.float32)).astype(o_ref.dtype)
        return carry

    lax.fori_loop(0, nq, finish, 0, unroll=2)


def _attn_call(q, k, vt, sgb, lq1, lk1, lq2, lk2, subln_col, lam_init):
    B, S, _ = q.shape
    nq = S // TQ
    qtab, ctab, n_full = _pair_tables(S)
    n_pairs = len(qtab)
    assert n_full >= 2
    zero = np.zeros((1,), np.int32)
    head = lambda b, h, *_: (b, 0, h)
    const2 = lambda b, h, *_: (0, 0)
    grid_spec = pltpu.PrefetchScalarGridSpec(
        num_scalar_prefetch=3,
        grid=(B, N_DIFF_HEADS),
        in_specs=[
            pl.BlockSpec((None, S, DIFF_VDIM), head),
            pl.BlockSpec((None, S, DIFF_VDIM), head),
            pl.BlockSpec((None, S // TK, DIFF_VDIM, TK), lambda b, h, *_: (b, 0, h, 0)),
            pl.BlockSpec((None, S, DIFF_VDIM), head),
            pl.BlockSpec((1, DIFF_QKDIM), const2),
            pl.BlockSpec((1, DIFF_QKDIM), const2),
            pl.BlockSpec((1, DIFF_QKDIM), const2),
            pl.BlockSpec((1, DIFF_QKDIM), const2),
            pl.BlockSpec((DIFF_VDIM, 1), const2),
        ],
        out_specs=pl.BlockSpec((None, S, DIFF_VDIM), head),
        scratch_shapes=[
            pltpu.VMEM((nq, 2 * TQ, DIFF_VDIM), jnp.bfloat16),
            pltpu.VMEM((S // TK, DIFF_VDIM + VT_PAD, TK), jnp.bfloat16),
            pltpu.VMEM((nq, 1, 2 * TQ), jnp.float32),
            pltpu.VMEM((nq, DIFF_VDIM + VT_PAD, 2 * TQ), jnp.float32),
            pltpu.VMEM((PIPE_BUFS, 1, 2 * TQ), jnp.float32),
        ] + [pltpu.VMEM((TK, 2 * TQ), jnp.float32)] * PIPE_BUFS,
    )
    return pl.pallas_call(
        functools.partial(_attn_kernel, lam_init, n_full, n_pairs),
        out_shape=jax.ShapeDtypeStruct((B, S, ATTN_WIDTH), jnp.bfloat16),
        grid_spec=grid_spec,
        compiler_params=pltpu.CompilerParams(
            dimension_semantics=("arbitrary", "arbitrary"),
            vmem_limit_bytes=VMEM_LIMIT),
        name="l0_diff_attn",
    )(qtab, ctab, zero, q, k, vt, sgb, lq1, lk1, lq2, lk2, subln_col)


def _out0_kernel(ya_ref, yb_ref, x_ref, w_ref, g_ref, o_ref):
    out = _dot(ya_ref[...], w_ref[0:POOL_WIDTH, :]) + _dot(yb_ref[...], w_ref[POOL_WIDTH:, :])
    o_ref[...] = x_ref[...] + _rms_norm(out, g_ref[...], EPS)


def _out0_call(ya, yb, x, w_out, g, layer):
    n = x.shape[0]
    tm = TM_OUT0
    tok = lambda i: (i, 0)
    const2 = lambda i: (0, 0)
    return pl.pallas_call(
        _out0_kernel,
        out_shape=jax.ShapeDtypeStruct((n, D_MODEL), jnp.float32),
        grid=(n // tm,),
        in_specs=[
            pl.BlockSpec((tm, POOL_WIDTH), tok),
            pl.BlockSpec((tm, ATTN_WIDTH), tok),
            pl.BlockSpec((tm, D_MODEL), tok),
            pl.BlockSpec((None, D_INNER, D_MODEL), lambda i: (layer, 0, 0),
                         pipeline_mode=pl.Buffered(1)),
            pl.BlockSpec((1, D_MODEL), const2),
        ],
        out_specs=pl.BlockSpec((tm, D_MODEL), tok),
        compiler_params=pltpu.CompilerParams(
            dimension_semantics=("arbitrary",),
            vmem_limit_bytes=VMEM_LIMIT),
        name="l0_out_proj",
    )(ya, yb, x, w_out, g)


def _l1_kernel(zero_ref, x_ref, g_ref, win_ref, lng_ref, lnb_ref, ws_ref, bs_ref, wout_ref, pg_ref,
               o_ref, u_ref, v_ref, gt_ref, vn_ref, y_ref, *raw_refs):
    tm = x_ref.shape[0]
    x = x_ref[...]
    h = _rms_norm(x, g_ref[...], EPS).astype(jnp.bfloat16)
    staged_rows = pl.ds(pl.multiple_of(zero_ref[0], tm), tm)
    nblk = D_INNER // L1_COL_BLOCK
    dests = ([(v_ref, _gelu_tanh, D_INNER)] * nblk + [(u_ref, _gelu_tanh, 0)] * nblk
             + [(gt_ref, _silu, 2 * D_INNER)] * nblk)

    def activate(i):
        dest_ref, act, _ = dests[i]
        j = i % nblk
        raw = raw_refs[i % L1_RAW_BUFS][staged_rows, :]
        dest_ref[:, j * L1_COL_BLOCK:(j + 1) * L1_COL_BLOCK] = act(raw)

    def layer_norm_v():
        vf = v_ref[...]
        mu = jnp.mean(vf, axis=-1, keepdims=True)
        vc = vf - mu
        var = jnp.mean(vc * vc, axis=-1, keepdims=True)
        vn_ref[...] = (vc * lax.rsqrt(var + EPS) * lng_ref[...] + lnb_ref[...]).astype(vn_ref.dtype)

    for i, (_, _, base) in enumerate(dests):
        col = base + (i % nblk) * L1_COL_BLOCK
        raw_refs[i % L1_RAW_BUFS][...] = _dot(h, win_ref[:, col:col + L1_COL_BLOCK])
        if i > 0:
            activate(i - 1)
        if i == nblk:
            layer_norm_v()
    activate(len(dests) - 1)

    tri_r = lax.broadcasted_iota(jnp.int32, (SGU_CHUNK, SGU_CHUNK), 0)
    tri_c = lax.broadcasted_iota(jnp.int32, (SGU_CHUNK, SGU_CHUNK), 1)
    tril = tri_c <= tri_r
    for g in range(SGU_GROUPS):
        cols = slice(g * SGU_GROUP_DIM, (g + 1) * SGU_GROUP_DIM)
        wg = jnp.where(tril, ws_ref[g], 0.0).astype(jnp.bfloat16)
        bias = bs_ref[:, g:g + 1]
        for c in range(tm // SGU_CHUNK):
            rows = slice(c * SGU_CHUNK, (c + 1) * SGU_CHUNK)
            mixed = _dot(wg, vn_ref[rows, cols]) + bias
            y_ref[rows, cols] = (u_ref[rows, cols] * mixed * gt_ref[rows, cols]).astype(y_ref.dtype)

    out = _dot(y_ref[...], wout_ref[...])
    o_ref[...] = x + _rms_norm(out, pg_ref[...], EPS)


def _l1_call(x, g, w_in, ln_g, ln_b, w_s, b_s_t, w_out, post_g, layer):
    n = x.shape[0]
    tm = TM_L1
    tok = lambda i: (i, 0)
    const2 = lambda i: (0, 0)
    return pl.pallas_call(
        _l1_kernel,
        out_shape=jax.ShapeDtypeStruct((n, D_MODEL), jnp.float32),
        grid=(n // tm,),
        in_specs=[
            pl.BlockSpec(memory_space=pltpu.SMEM),
            pl.BlockSpec((tm, D_MODEL), tok),
            pl.BlockSpec((1, D_MODEL), const2),
            pl.BlockSpec((None, D_MODEL, IN_COLS), lambda i: (layer, 0, 0),
                         pipeline_mode=pl.Buffered(1)),
            pl.BlockSpec((1, D_INNER), const2),
            pl.BlockSpec((1, D_INNER), const2),
            pl.BlockSpec((SGU_GROUPS, SGU_CHUNK, SGU_CHUNK), lambda i: (0, 0, 0)),
            pl.BlockSpec((SGU_CHUNK, SGU_GROUPS), const2),
            pl.BlockSpec((None, D_INNER, D_MODEL), lambda i: (layer, 0, 0),
                         pipeline_mode=pl.Buffered(1)),
            pl.BlockSpec((1, D_MODEL), const2),
        ],
        out_specs=pl.BlockSpec((tm, D_MODEL), tok),
        scratch_shapes=[
            pltpu.VMEM((tm, D_INNER), jnp.float32),
            pltpu.VMEM((tm, D_INNER), jnp.float32),
            pltpu.VMEM((tm, D_INNER), jnp.float32),
            pltpu.VMEM((tm, D_INNER), jnp.bfloat16),
            pltpu.VMEM((tm, D_INNER), jnp.bfloat16),
        ] + [pltpu.VMEM((tm, L1_COL_BLOCK), jnp.float32)] * L1_RAW_BUFS,
        compiler_params=pltpu.CompilerParams(
            dimension_semantics=("arbitrary",),
            vmem_limit_bytes=VMEM_LIMIT),
        name="l1_sgu_layer",
    )(jnp.zeros((1,), jnp.int32), x, g, w_in, ln_g, ln_b, w_s, b_s_t, w_out, post_g)


def _rope_lane_tables():
    half = ROPE_DIM // 2
    inv_freq = jnp.power(ROPE_THETA, -jnp.arange(half, dtype=jnp.float32) * 2.0 / ROPE_DIM)
    d = jnp.arange(LANES) % DIFF_QKDIM
    invf = jnp.where(d < ROPE_DIM, inv_freq[d % half], 0.0).astype(jnp.float32)
    m1 = jnp.where(d < half, -1.0, 0.0).astype(jnp.float32)
    m2 = jnp.where((d >= half) & (d < ROPE_DIM), 1.0, 0.0).astype(jnp.float32)
    return invf[None, :], m1[None, :], m2[None, :]


def kernel(x, positions, pre_norm, post_norm, w_in, w_out, pool_w, pool_scale, lam_q1, lam_k1, lam_q2, lam_k2, diff_subln, sgu_ln_g, sgu_ln_b, sgu_w, sgu_b):
    B, S, D = x.shape
    bf16 = jnp.bfloat16
    w_in_b = w_in.astype(bf16)
    w_out_b = w_out.astype(bf16)
    invf, m1, m2 = _rope_lane_tables()
    pos_f = positions.astype(jnp.float32)[:, :, None]

    q, k, vt, ya, sgb = _in0_call(x, pos_f, pre_norm[0:1], w_in_b, pool_w[0].astype(bf16),
                                  pool_scale[0:1], invf, m1, m2, layer=0)
    lam_init = 0.8 - 0.6 * math.exp(-0.3 * 0)
    yb = _attn_call(q, k, vt, sgb, lam_q1[0:1], lam_k1[0:1], lam_q2[0:1], lam_k2[0:1],
                    diff_subln[0].reshape(DIFF_VDIM, 1), lam_init)
    n = B * S
    x1 = _out0_call(ya.reshape(n, POOL_WIDTH), yb.reshape(n, ATTN_WIDTH), x.reshape(n, D),
                    w_out_b, post_norm[0:1], layer=0)

    x2 = _l1_call(x1, pre_norm[1:2], w_in_b, sgu_ln_g[0:1], sgu_ln_b[0:1], sgu_w[0],
                  sgu_b[0].T, w_out_b, post_norm[1:2], layer=1)
    return x2.reshape(B, S, D)
```

```python
import functools
import math

import jax
import jax.numpy as jnp
import numpy as np
from jax import lax
from jax.experimental import pallas as pl
from jax.experimental.pallas import tpu as pltpu

D_MODEL = 1024
D_INNER = 2048
IN_COLS = 3 * D_INNER
POOL_WIDTH = 1024
POOL_GROUPS = 4
POOL_GROUP_DIM = 256
POOL_WINDOWS = (2, 4, 8, 16)
POOL_HALO = 16
ATTN_WIDTH = 1024
N_DIFF_HEADS = 8
DIFF_VDIM = 128
DIFF_QKDIM = 64
ROPE_THETA = 500000.0
ROPE_DIM = 16
SGU_CHUNK = 128
SGU_GROUPS = 8
SGU_GROUP_DIM = 256
EPS = 1e-6
SUBLN_EPS = 1e-5
LANES = 128
LOG2_E = math.log2(math.e)

TQ = 512
TK = 512
TM_IN0 = TK
VT_PAD = 16
PIPE_BUFS = 3
PIPE_UNROLL = 12
assert PIPE_UNROLL % PIPE_BUFS == 0
TM_OUT0 = 512
TM_L1 = 512
L1_COL_BLOCK = 512
L1_RAW_BUFS = 5
VMEM_LIMIT = 56 * 1024 * 1024

_NT = (((1,), (1,)), ((), ()))


def _rms_norm(x, g, eps):
    return x * lax.rsqrt(jnp.mean(x * x, axis=-1, keepdims=True) + eps) * g


def _silu(x):
    return x / (1.0 + jnp.exp(-x))


def _gelu_tanh(x):
    c = math.sqrt(2.0 / math.pi)
    return 0.5 * x * (1.0 + jnp.tanh(c * (x + 0.044715 * (x * x * x))))


def _dot(a, b):
    return jnp.dot(a, b, preferred_element_type=jnp.float32)


def _in0_kernel(x_ref, pos_ref, g_ref, w_ref, pw_ref, ps_ref, invf_ref, m1_ref, m2_ref,
                q_ref, k_ref, vt_ref, ya_ref, sgb_ref, aext_ref):
    i = pl.program_id(1)
    tm = x_ref.shape[0]
    h = _rms_norm(x_ref[...], g_ref[...], EPS).astype(jnp.bfloat16)

    def proj(col):
        return _dot(h, w_ref[:, col * 1024:(col + 1) * 1024])

    ang = pos_ref[...] * invf_ref[...]
    cs = jnp.cos(ang)
    sn = jnp.sin(ang)
    s_lo = sn * m1_ref[...]
    s_hi = sn * m2_ref[...]
    half = ROPE_DIM // 2

    def rope_store(t, out_ref, scale):
        for c in range(ATTN_WIDTH // LANES):
            tc = t[:, c * LANES:(c + 1) * LANES]
            up = pltpu.roll(tc, LANES - half, axis=1)
            dn = pltpu.roll(tc, half, axis=1)
            r = tc * cs + up * s_lo + dn * s_hi
            if scale != 1.0:
                r = r * scale
            out_ref[:, c * LANES:(c + 1) * LANES] = r.astype(out_ref.dtype)

    rope_store(proj(1), q_ref, DIFF_QKDIM ** -0.5 * LOG2_E)
    rope_store(proj(2), k_ref, 1.0)
    vt_ref[...] = proj(3).T.astype(vt_ref.dtype)
    sgb_ref[...] = _silu(proj(5)).astype(sgb_ref.dtype)

    @pl.when(i == 0)
    def _():
        aext_ref[0:POOL_HALO, :] = jnp.zeros((POOL_HALO, POOL_WIDTH), jnp.float32)

    @pl.when(i > 0)
    def _():
        aext_ref[0:POOL_HALO, :] = aext_ref[tm:tm + POOL_HALO, :]

    aext_ref[POOL_HALO:, :] = proj(0)
    gate_a = _silu(proj(4))
    t_idx = i * tm + lax.broadcasted_iota(jnp.int32, (tm, 1), 0)
    for g, w in enumerate(POOL_WINDOWS):
        cols = slice(g * POOL_GROUP_DIM, (g + 1) * POOL_GROUP_DIM)
        e = aext_ref[:, cols]
        acc = e
        span = 1
        while span < w:
            acc = acc + pltpu.roll(acc, span, axis=0)
            span *= 2
        cnt = jnp.minimum(t_idx + 1, w).astype(jnp.float32)
        pooled = acc[POOL_HALO:, :] / cnt - e[POOL_HALO:, :]
        y = _dot(pooled.astype(jnp.bfloat16), pw_ref[g]) * ps_ref[:, cols]
        ya_ref[:, cols] = (y * gate_a[:, cols]).astype(ya_ref.dtype)


def _in0_call(x, pos_f, g, w_in, pool_w, pool_scale, invf, m1, m2, layer):
    B, S, _ = x.shape
    tm = TM_IN0
    tok = lambda b, i: (b, i, 0)
    const2 = lambda b, i: (0, 0)
    out = jax.ShapeDtypeStruct((B, S, 1024), jnp.bfloat16)
    out_vt = jax.ShapeDtypeStruct((B, S // tm, ATTN_WIDTH, tm), jnp.bfloat16)
    spec = pl.BlockSpec((None, tm, 1024), tok)
    spec_vt = pl.BlockSpec((None, None, ATTN_WIDTH, tm), lambda b, i: (b, i, 0, 0))
    return pl.pallas_call(
        _in0_kernel,
        out_shape=[out, out, out_vt, out, out],
        grid=(B, S // tm),
        in_specs=[
            pl.BlockSpec((None, tm, D_MODEL), tok),
            pl.BlockSpec((None, tm, 1), tok),
            pl.BlockSpec((1, D_MODEL), const2),
            pl.BlockSpec((None, D_MODEL, IN_COLS), lambda b, i: (layer, 0, 0),
                         pipeline_mode=pl.Buffered(1)),
            pl.BlockSpec((POOL_GROUPS, POOL_GROUP_DIM, POOL_GROUP_DIM), lambda b, i: (0, 0, 0)),
            pl.BlockSpec((1, POOL_WIDTH), const2),
            pl.BlockSpec((1, LANES), const2),
            pl.BlockSpec((1, LANES), const2),
            pl.BlockSpec((1, LANES), const2),
        ],
        out_specs=[spec, spec, spec_vt, spec, spec],
        scratch_shapes=[pltpu.VMEM((POOL_HALO + tm, POOL_WIDTH), jnp.float32)],
        compiler_params=pltpu.CompilerParams(
            dimension_semantics=("arbitrary", "arbitrary"),
            vmem_limit_bytes=VMEM_LIMIT),
        name="l0_in_proj",
    )(x, pos_f, g, w_in, pool_w, pool_scale, invf, m1, m2)


def _pair_tables(seq):
    nq = seq // TQ
    n_blocks = [(qi * TQ + TQ - 1) // TK + 1 for qi in range(nq)]
    full = [(qi, c) for qi in range(nq) for c in range(n_blocks[qi] - 1)]
    diag = [(qi, n_blocks[qi] - 1) for qi in range(nq)]
    pairs = np.array(full + diag, np.int32)
    return pairs[:, 0], pairs[:, 1], len(full)


def _attn_kernel(lam_init, n_full, n_pairs, qtab_ref, ctab_ref, zero_ref,
                 q_ref, k_ref, vt_ref, sgb_ref, lq1_ref, lk1_ref, lq2_ref, lk2_ref, sub_ref,
                 o_ref, qs_ref, vte_ref, m_ref, acc_ref, cm_ref, *s_refs):
    staged_rows = pl.ds(pl.multiple_of(zero_ref[0], TK), TK)
    nq = q_ref.shape[0] // TQ
    lane = lax.broadcasted_iota(jnp.int32, (TQ, LANES), 1)

    def stack_queries(i, carry):
        q = q_ref[pl.ds(pl.multiple_of(i * TQ, TQ), TQ), :]
        zero = jnp.zeros_like(q)
        qs_ref[i, 0:TQ, :] = jnp.where(lane < DIFF_QKDIM, q, zero)
        qs_ref[i, TQ:, :] = jnp.where(lane >= DIFF_QKDIM, q, zero)
        return carry

    lax.fori_loop(0, nq, stack_queries, 0)
    for c in range(vt_ref.shape[0]):
        vte_ref[c, 0:DIFF_VDIM, :] = vt_ref[c]
        vte_ref[c, DIFF_VDIM:, :] = jnp.ones((VT_PAD, TK), vte_ref.dtype)
    m_ref[...] = jnp.full(m_ref.shape, -jnp.inf, jnp.float32)
    acc_ref[...] = jnp.zeros(acc_ref.shape, jnp.float32)

    def scores(t, buf, masked):
        qi = qtab_ref[t]
        start = pl.multiple_of(ctab_ref[t] * TK, TK)
        kc = k_ref[pl.ds(start, TK), :]
        s = lax.dot_general(kc, qs_ref[qi], _NT, preferred_element_type=jnp.float32)
        if masked:
            key = lax.broadcasted_iota(jnp.int32, (TK, 2 * TQ), 0)
            col = lax.broadcasted_iota(jnp.int32, (TK, 2 * TQ), 1)
            qpos = qi * TQ + jnp.where(col >= TQ, col - TQ, col)
            s = jnp.where(start + key <= qpos, s, -jnp.inf)
        s_refs[buf][...] = s
        cm_ref[buf] = jnp.max(s, axis=0, keepdims=True)

    def softmax(t, buf):
        qi = qtab_ref[t]
        m_old = m_ref[qi]
        m_new = jnp.maximum(m_old, cm_ref[buf])
        m_ref[qi] = m_new
        s = s_refs[buf][staged_rows, :]
        p = jnp.exp2((s - m_new).astype(jnp.bfloat16))
        acc_ref[qi] = jnp.exp2(m_old - m_new) * acc_ref[qi] + _dot(vte_ref[ctab_ref[t]], p)

    def stage(t, phase, masked):
        scores(t, phase % PIPE_BUFS, masked)
        softmax(t - 1, (phase - 1) % PIPE_BUFS)

    def run_stages(first, count, masked):
        trips, rest = divmod(count, PIPE_UNROLL)

        def body(j, carry):
            for u in range(PIPE_UNROLL):
                stage(first + PIPE_UNROLL * j + u, first + u, masked)
            return carry

        lax.fori_loop(0, trips, body, 0)
        for u in range(rest):
            stage(first + PIPE_UNROLL * trips + u, first + u, masked)

    scores(0, 0, False)
    run_stages(1, n_full - 1, False)
    run_stages(n_full, n_pairs - n_full, True)
    softmax(n_pairs - 1, (n_pairs - 1) % PIPE_BUFS)

    lam = (jnp.exp(jnp.sum(lq1_ref[...] * lk1_ref[...], axis=-1, keepdims=True))
           - jnp.exp(jnp.sum(lq2_ref[...] * lk2_ref[...], axis=-1, keepdims=True))
           + lam_init)
    sub = sub_ref[...] * (1.0 - lam_init)

    def finish(i, carry):
        rows = pl.ds(pl.multiple_of(i * TQ, TQ), TQ)
        r = 1.0 / acc_ref[i, DIFF_VDIM:DIFF_VDIM + 1, :]
        acc = acc_ref[i, 0:DIFF_VDIM, :]
        ot = acc[:, 0:TQ] * r[:, 0:TQ] - lam * (acc[:, TQ:] * r[:, TQ:])
        ms = jnp.mean(ot * ot, axis=0, keepdims=True)
        ot = ot * lax.rsqrt(ms + SUBLN_EPS) * sub
        o_ref[rows, :] = (ot.T * sgb_ref[rows, :].astype(jnp.float32)).astype(o_ref.dtype)
        return carry

    lax.fori_loop(0, nq, finish, 0, unroll=2)


def _attn_call(q, k, vt, sgb, lq1, lk1, lq2, lk2, subln_col, lam_init):
    B, S, _ = q.shape
    nq = S // TQ
    qtab, ctab, n_full = _pair_tables(S)
    n_pairs = len(qtab)
    assert n_full >= 2
    zero = np.zeros((1,), np.int32)
    head = lambda b, h, *_: (b, 0, h)
    const2 = lambda b, h, *_: (0, 0)
    grid_spec = pltpu.PrefetchScalarGridSpec(
        num_scalar_prefetch=3,
        grid=(B, N_DIFF_HEADS),
        in_specs=[
            pl.BlockSpec((None, S, DIFF_VDIM), head),
            pl.BlockSpec((None, S, DIFF_VDIM), head),
            pl.BlockSpec((None, S // TK, DIFF_VDIM, TK), lambda b, h, *_: (b, 0, h, 0)),
            pl.BlockSpec((None, S, DIFF_VDIM), head),
            pl.BlockSpec((1, DIFF_QKDIM), const2),
            pl.BlockSpec((1, DIFF_QKDIM), const2),
            pl.BlockSpec((1, DIFF_QKDIM), const2),
            pl.BlockSpec((1, DIFF_QKDIM), const2),
            pl.BlockSpec((DIFF_VDIM, 1), const2),
        ],
        out_specs=pl.BlockSpec((None, S, DIFF_VDIM), head),
        scratch_shapes=[
            pltpu.VMEM((nq, 2 * TQ, DIFF_VDIM), jnp.bfloat16),
            pltpu.VMEM((S // TK, DIFF_VDIM + VT_PAD, TK), jnp.bfloat16),
            pltpu.VMEM((nq, 1, 2 * TQ), jnp.float32),
            pltpu.VMEM((nq, DIFF_VDIM + VT_PAD, 2 * TQ), jnp.float32),
            pltpu.VMEM((PIPE_BUFS, 1, 2 * TQ), jnp.float32),
        ] + [pltpu.VMEM((TK, 2 * TQ), jnp.float32)] * PIPE_BUFS,
    )
    return pl.pallas_call(
        functools.partial(_attn_kernel, lam_init, n_full, n_pairs),
        out_shape=jax.ShapeDtypeStruct((B, S, ATTN_WIDTH), jnp.bfloat16),
        grid_spec=grid_spec,
        compiler_params=pltpu.CompilerParams(
            dimension_semantics=("arbitrary", "arbitrary"),
            vmem_limit_bytes=VMEM_LIMIT),
        name="l0_diff_attn",
    )(qtab, ctab, zero, q, k, vt, sgb, lq1, lk1, lq2, lk2, subln_col)


def _out0_kernel(ya_ref, yb_ref, x_ref, w_ref, g_ref, o_ref):
    out = _dot(ya_ref[...], w_ref[0:POOL_WIDTH, :]) + _dot(yb_ref[...], w_ref[POOL_WIDTH:, :])
    o_ref[...] = x_ref[...] + _rms_norm(out, g_ref[...], EPS)


def _out0_call(ya, yb, x, w_out, g, layer):
    n = x.shape[0]
    tm = TM_OUT0
    tok = lambda i: (i, 0)
    const2 = lambda i: (0, 0)
    return pl.pallas_call(
        _out0_kernel,
        out_shape=jax.ShapeDtypeStruct((n, D_MODEL), jnp.float32),
        grid=(n // tm,),
        in_specs=[
            pl.BlockSpec((tm, POOL_WIDTH), tok),
            pl.BlockSpec((tm, ATTN_WIDTH), tok),
            pl.BlockSpec((tm, D_MODEL), tok),
            pl.BlockSpec((None, D_INNER, D_MODEL), lambda i: (layer, 0, 0),
                         pipeline_mode=pl.Buffered(1)),
            pl.BlockSpec((1, D_MODEL), const2),
        ],
        out_specs=pl.BlockSpec((tm, D_MODEL), tok),
        compiler_params=pltpu.CompilerParams(
            dimension_semantics=("arbitrary",),
            vmem_limit_bytes=VMEM_LIMIT),
        name="l0_out_proj",
    )(ya, yb, x, w_out, g)


def _l1_kernel(zero_ref, x_ref, g_ref, win_ref, lng_ref, lnb_ref, ws_ref, bs_ref, wout_ref, pg_ref,
               o_ref, u_ref, v_ref, gt_ref, vn_ref, y_ref, *raw_refs):
    tm = x_ref.shape[0]
    x = x_ref[...]
    h = _rms_norm(x, g_ref[...], EPS).astype(jnp.bfloat16)
    staged_rows = pl.ds(pl.multiple_of(zero_ref[0], tm), tm)
    nblk = D_INNER // L1_COL_BLOCK
    dests = ([(v_ref, _gelu_tanh, D_INNER)] * nblk + [(u_ref, _gelu_tanh, 0)] * nblk
             + [(gt_ref, _silu, 2 * D_INNER)] * nblk)

    def activate(i):
        dest_ref, act, _ = dests[i]
        j = i % nblk
        raw = raw_refs[i % L1_RAW_BUFS][staged_rows, :]
        dest_ref[:, j * L1_COL_BLOCK:(j + 1) * L1_COL_BLOCK] = act(raw)

    def layer_norm_v():
        vf = v_ref[...]
        mu = jnp.mean(vf, axis=-1, keepdims=True)
        vc = vf - mu
        var = jnp.mean(vc * vc, axis=-1, keepdims=True)
        vn_ref[...] = (vc * lax.rsqrt(var + EPS) * lng_ref[...] + lnb_ref[...]).astype(vn_ref.dtype)

    for i, (_, _, base) in enumerate(dests):
        col = base + (i % nblk) * L1_COL_BLOCK
        raw_refs[i % L1_RAW_BUFS][...] = _dot(h, win_ref[:, col:col + L1_COL_BLOCK])
        if i > 0:
            activate(i - 1)
        if i == nblk:
            layer_norm_v()
    activate(len(dests) - 1)

    tri_r = lax.broadcasted_iota(jnp.int32, (SGU_CHUNK, SGU_CHUNK), 0)
    tri_c = lax.broadcasted_iota(jnp.int32, (SGU_CHUNK, SGU_CHUNK), 1)
    tril = tri_c <= tri_r
    for g in range(SGU_GROUPS):
        cols = slice(g * SGU_GROUP_DIM, (g + 1) * SGU_GROUP_DIM)
        wg = jnp.where(tril, ws_ref[g], 0.0).astype(jnp.bfloat16)
        bias = bs_ref[:, g:g + 1]
        for c in range(tm // SGU_CHUNK):
            rows = slice(c * SGU_CHUNK, (c + 1) * SGU_CHUNK)
            mixed = _dot(wg, vn_ref[rows, cols]) + bias
            y_ref[rows, cols] = (u_ref[rows, cols] * mixed * gt_ref[rows, cols]).astype(y_ref.dtype)

    out = _dot(y_ref[...], wout_ref[...])
    o_ref[...] = x + _rms_norm(out, pg_ref[...], EPS)


def _l1_call(x, g, w_in, ln_g, ln_b, w_s, b_s_t, w_out, post_g, layer):
    n = x.shape[0]
    tm = TM_L1
    tok = lambda i: (i, 0)
    const2 = lambda i: (0, 0)
    return pl.pallas_call(
        _l1_kernel,
        out_shape=jax.ShapeDtypeStruct((n, D_MODEL), jnp.float32),
        grid=(n // tm,),
        in_specs=[
            pl.BlockSpec(memory_space=pltpu.SMEM),
            pl.BlockSpec((tm, D_MODEL), tok),
            pl.BlockSpec((1, D_MODEL), const2),
            pl.BlockSpec((None, D_MODEL, IN_COLS), lambda i: (layer, 0, 0),
                         pipeline_mode=pl.Buffered(1)),
            pl.BlockSpec((1, D_INNER), const2),
            pl.BlockSpec((1, D_INNER), const2),
            pl.BlockSpec((SGU_GROUPS, SGU_CHUNK, SGU_CHUNK), lambda i: (0, 0, 0)),
            pl.BlockSpec((SGU_CHUNK, SGU_GROUPS), const2),
            pl.BlockSpec((None, D_INNER, D_MODEL), lambda i: (layer, 0, 0),
                         pipeline_mode=pl.Buffered(1)),
            pl.BlockSpec((1, D_MODEL), const2),
        ],
        out_specs=pl.BlockSpec((tm, D_MODEL), tok),
        scratch_shapes=[
            pltpu.VMEM((tm, D_INNER), jnp.float32),
            pltpu.VMEM((tm, D_INNER), jnp.float32),
            pltpu.VMEM((tm, D_INNER), jnp.float32),
            pltpu.VMEM((tm, D_INNER), jnp.bfloat16),
            pltpu.VMEM((tm, D_INNER), jnp.bfloat16),
        ] + [pltpu.VMEM((tm, L1_COL_BLOCK), jnp.float32)] * L1_RAW_BUFS,
        compiler_params=pltpu.CompilerParams(
            dimension_semantics=("arbitrary",),
            vmem_limit_bytes=VMEM_LIMIT),
        name="l1_sgu_layer",
    )(jnp.zeros((1,), jnp.int32), x, g, w_in, ln_g, ln_b, w_s, b_s_t, w_out, post_g)


def _rope_lane_tables():
    half = ROPE_DIM // 2
    inv_freq = jnp.power(ROPE_THETA, -jnp.arange(half, dtype=jnp.float32) * 2.0 / ROPE_DIM)
    d = jnp.arange(LANES) % DIFF_QKDIM
    invf = jnp.where(d < ROPE_DIM, inv_freq[d % half], 0.0).astype(jnp.float32)
    m1 = jnp.where(d < half, -1.0, 0.0).astype(jnp.float32)
    m2 = jnp.where((d >= half) & (d < ROPE_DIM), 1.0, 0.0).astype(jnp.float32)
    return invf[None, :], m1[None, :], m2[None, :]


def kernel(x, positions, pre_norm, post_norm, w_in, w_out, pool_w, pool_scale, lam_q1, lam_k1, lam_q2, lam_k2, diff_subln, sgu_ln_g, sgu_ln_b, sgu_w, sgu_b):
    B, S, D = x.shape
    bf16 = jnp.bfloat16
    w_in_b = w_in.astype(bf16)
    w_out_b = w_out.astype(bf16)
    invf, m1, m2 = _rope_lane_tables()
    pos_f = positions.astype(jnp.float32)[:, :, None]

    q, k, vt, ya, sgb = _in0_call(x, pos_f, pre_norm[0:1], w_in_b, pool_w[0].astype(bf16),
                                  pool_scale[0:1], invf, m1, m2, layer=0)
    lam_init = 0.8 - 0.6 * math.exp(-0.3 * 0)
    yb = _attn_call(q, k, vt, sgb, lam_q1[0:1], lam_k1[0:1], lam_q2[0:1], lam_k2[0:1],
                    diff_subln[0].reshape(DIFF_VDIM, 1), lam_init)
    n = B * S
    x1 = _out0_call(ya.reshape(n, POOL_WIDTH), yb.reshape(n, ATTN_WIDTH), x.reshape(n, D),
                    w_out_b, post_norm[0:1], layer=0)

    x2 = _l1_call(x1, pre_norm[1:2], w_in_b, sgu_ln_g[0:1], sgu_ln_b[0:1], sgu_w[0],
                  sgu_b[0].T, w_out_b, post_norm[1:2], layer=1)
    return x2.reshape(B, S, D)
```

```python
import functools
import math

import jax
import jax.numpy as jnp
import numpy as np
from jax import lax
from jax.experimental import pallas as pl
from jax.experimental.pallas import tpu as pltpu

D_MODEL = 1024
D_INNER = 2048
IN_COLS = 3 * D_INNER
POOL_WIDTH = 1024
POOL_GROUPS = 4
POOL_GROUP_DIM = 256
POOL_WINDOWS = (2, 4, 8, 16)
POOL_HALO = 16
ATTN_WIDTH = 1024
N_DIFF_HEADS = 8
DIFF_VDIM = 128
DIFF_QKDIM = 64
ROPE_THETA = 500000.0
ROPE_DIM = 16
SGU_CHUNK = 128
SGU_GROUPS = 8
SGU_GROUP_DIM = 256
EPS = 1e-6
SUBLN_EPS = 1e-5
LANES = 128
LOG2_E = math.log2(math.e)

TQ = 512
TK = 512
TM_IN0 = TK
VT_PAD = 16
PIPE_BUFS = 3
PIPE_UNROLL = 12
assert PIPE_UNROLL % PIPE_BUFS == 0
TM_OUT0 = 512
TM_L1 = 512
L1_COL_BLOCK = 512
L1_RAW_BUFS = 5
VMEM_LIMIT = 56 * 1024 * 1024

_NT = (((1,), (1,)), ((), ()))


def _rms_norm(x, g, eps):
    return x * lax.rsqrt(jnp.mean(x * x, axis=-1, keepdims=True) + eps) * g


def _silu(x):
    return x / (1.0 + jnp.exp(-x))


def _gelu_tanh(x):
    c = math.sqrt(2.0 / math.pi)
    return 0.5 * x * (1.0 + jnp.tanh(c * (x + 0.044715 * (x * x * x))))


def _dot(a, b):
    return jnp.dot(a, b, preferred_element_type=jnp.float32)


def _in0_kernel(x_ref, pos_ref, g_ref, w_ref, pw_ref, ps_ref, invf_ref, m1_ref, m2_ref,
                q_ref, k_ref, vt_ref, ya_ref, sgb_ref, aext_ref):
    i = pl.program_id(1)
    tm = x_ref.shape[0]

    @pl.when(i == 0)
    def _():
        aext_ref[0:POOL_HALO, :] = jnp.zeros((POOL_HALO, POOL_WIDTH), jnp.float32)

    @pl.when(i > 0)
    def _():
        aext_ref[0:POOL_HALO, :] = aext_ref[tm:tm + POOL_HALO, :]

    h = _rms_norm(x_ref[...], g_ref[...], EPS).astype(jnp.bfloat16)

    def proj(col):
        return _dot(h, w_ref[:, col * 1024:(col + 1) * 1024])

    vt_ref[...] = proj(3).T.astype(vt_ref.dtype)
    aext_ref[POOL_HALO:, :] = proj(0)
    gate_a = _silu(proj(4))
    t_idx = i * tm + lax.broadcasted_iota(jnp.int32, (tm, 1), 0)
    for g, w in enumerate(POOL_WINDOWS):
        cols = slice(g * POOL_GROUP_DIM, (g + 1) * POOL_GROUP_DIM)
        e = aext_ref[:, cols]
        acc = e
        span = 1
        while span < w:
            acc = acc + pltpu.roll(acc, span, axis=0)
            span *= 2
        cnt = jnp.minimum(t_idx + 1, w).astype(jnp.float32)
        pooled = acc[POOL_HALO:, :] / cnt - e[POOL_HALO:, :]
        y = _dot(pooled.astype(jnp.bfloat16), pw_ref[g]) * ps_ref[:, cols]
        ya_ref[:, cols] = (y * gate_a[:, cols]).astype(ya_ref.dtype)

    ang = pos_ref[...] * invf_ref[...]
    cs = jnp.cos(ang)
    sn = jnp.sin(ang)
    s_lo = sn * m1_ref[...]
    s_hi = sn * m2_ref[...]
    half = ROPE_DIM // 2

    def rope_store(t, out_ref, scale):
        for c in range(ATTN_WIDTH // LANES):
            tc = t[:, c * LANES:(c + 1) * LANES]
            up = pltpu.roll(tc, LANES - half, axis=1)
            dn = pltpu.roll(tc, half, axis=1)
            r = tc * cs + up * s_lo + dn * s_hi
            if scale != 1.0:
                r = r * scale
            out_ref[:, c * LANES:(c + 1) * LANES] = r.astype(out_ref.dtype)

    rope_store(proj(1), q_ref, DIFF_QKDIM ** -0.5 * LOG2_E)
    rope_store(proj(2), k_ref, 1.0)
    sgb_ref[...] = _silu(proj(5)).astype(sgb_ref.dtype)


def _in0_call(x, pos_f, g, w_in, pool_w, pool_scale, invf, m1, m2, layer):
    B, S, _ = x.shape
    tm = TM_IN0
    tok = lambda b, i: (b, i, 0)
    const2 = lambda b, i: (0, 0)
    out = jax.ShapeDtypeStruct((B, S, 1024), jnp.bfloat16)
    out_vt = jax.ShapeDtypeStruct((B, S // tm, ATTN_WIDTH, tm), jnp.bfloat16)
    spec = pl.BlockSpec((None, tm, 1024), tok)
    spec_vt = pl.BlockSpec((None, None, ATTN_WIDTH, tm), lambda b, i: (b, i, 0, 0))
    return pl.pallas_call(
        _in0_kernel,
        out_shape=[out, out, out_vt, out, out],
        grid=(B, S // tm),
        in_specs=[
            pl.BlockSpec((None, tm, D_MODEL), tok),
            pl.BlockSpec((None, tm, 1), tok),
            pl.BlockSpec((1, D_MODEL), const2),
            pl.BlockSpec((None, D_MODEL, IN_COLS), lambda b, i: (layer, 0, 0),
                         pipeline_mode=pl.Buffered(1)),
            pl.BlockSpec((POOL_GROUPS, POOL_GROUP_DIM, POOL_GROUP_DIM), lambda b, i: (0, 0, 0)),
            pl.BlockSpec((1, POOL_WIDTH), const2),
            pl.BlockSpec((1, LANES), const2),
            pl.BlockSpec((1, LANES), const2),
            pl.BlockSpec((1, LANES), const2),
        ],
        out_specs=[spec, spec, spec_vt, spec, spec],
        scratch_shapes=[pltpu.VMEM((POOL_HALO + tm, POOL_WIDTH), jnp.float32)],
        compiler_params=pltpu.CompilerParams(
            dimension_semantics=("arbitrary", "arbitrary"),
            vmem_limit_bytes=VMEM_LIMIT),
        name="l0_in_proj",
    )(x, pos_f, g, w_in, pool_w, pool_scale, invf, m1, m2)


def _pair_tables(seq):
    nq = seq // TQ
    n_blocks = [(qi * TQ + TQ - 1) // TK + 1 for qi in range(nq)]
    full = [(qi, c) for qi in range(nq) for c in range(n_blocks[qi] - 1)]
    diag = [(qi, n_blocks[qi] - 1) for qi in range(nq)]
    pairs = np.array(full + diag, np.int32)
    return pairs[:, 0], pairs[:, 1], len(full)


def _attn_kernel(lam_init, n_full, n_pairs, qtab_ref, ctab_ref, zero_ref,
                 q_ref, k_ref, vt_ref, sgb_ref, lq1_ref, lk1_ref, lq2_ref, lk2_ref, sub_ref,
                 o_ref, qs_ref, vte_ref, m_ref, acc_ref, cm_ref, *s_refs):
    staged_rows = pl.ds(pl.multiple_of(zero_ref[0], TK), TK)
    nq = q_ref.shape[0] // TQ
    lane = lax.broadcasted_iota(jnp.int32, (TQ, LANES), 1)

    def stack_queries(i, carry):
        q = q_ref[pl.ds(pl.multiple_of(i * TQ, TQ), TQ), :]
        zero = jnp.zeros_like(q)
        qs_ref[i, 0:TQ, :] = jnp.where(lane < DIFF_QKDIM, q, zero)
        qs_ref[i, TQ:, :] = jnp.where(lane >= DIFF_QKDIM, q, zero)
        return carry

    lax.fori_loop(0, nq, stack_queries, 0)
    for c in range(vt_ref.shape[0]):
        vte_ref[c, 0:DIFF_VDIM, :] = vt_ref[c]
        vte_ref[c, DIFF_VDIM:, :] = jnp.ones((VT_PAD, TK), vte_ref.dtype)
    m_ref[...] = jnp.full(m_ref.shape, -jnp.inf, jnp.float32)
    acc_ref[...] = jnp.zeros(acc_ref.shape, jnp.float32)

    def scores(t, buf, masked):
        qi = qtab_ref[t]
        start = pl.multiple_of(ctab_ref[t] * TK, TK)
        kc = k_ref[pl.ds(start, TK), :]
        s = lax.dot_general(kc, qs_ref[qi], _NT, preferred_element_type=jnp.float32)
        if masked:
            key = lax.broadcasted_iota(jnp.int32, (TK, 2 * TQ), 0)
            col = lax.broadcasted_iota(jnp.int32, (TK, 2 * TQ), 1)
            qpos = qi * TQ + jnp.where(col >= TQ, col - TQ, col)
            s = jnp.where(start + key <= qpos, s, -jnp.inf)
        s_refs[buf][...] = s
        cm_ref[buf] = jnp.max(s, axis=0, keepdims=True)

    def softmax(t, buf):
        qi = qtab_ref[t]
        m_old = m_ref[qi]
        m_new = jnp.maximum(m_old, cm_ref[buf])
        m_ref[qi] = m_new
        s = s_refs[buf][staged_rows, :]
        p = jnp.exp2((s - m_new).astype(jnp.bfloat16))
        acc_ref[qi] = jnp.exp2(m_old - m_new) * acc_ref[qi] + _dot(vte_ref[ctab_ref[t]], p)

    def stage(t, phase, masked):
        scores(t, phase % PIPE_BUFS, masked)
        softmax(t - 1, (phase - 1) % PIPE_BUFS)

    def run_stages(first, count, masked):
        trips, rest = divmod(count, PIPE_UNROLL)

        def body(j, carry):
            for u in range(PIPE_UNROLL):
                stage(first + PIPE_UNROLL * j + u, first + u, masked)
            return carry

        lax.fori_loop(0, trips, body, 0)
        for u in range(rest):
            stage(first + PIPE_UNROLL * trips + u, first + u, masked)

    scores(0, 0, False)
    run_stages(1, n_full - 1, False)
    run_stages(n_full, n_pairs - n_full, True)
    softmax(n_pairs - 1, (n_pairs - 1) % PIPE_BUFS)

    lam = (jnp.exp(jnp.sum(lq1_ref[...] * lk1_ref[...], axis=-1, keepdims=True))
           - jnp.exp(jnp.sum(lq2_ref[...] * lk2_ref[...], axis=-1, keepdims=True))
           + lam_init)
    sub = sub_ref[...] * (1.0 - lam_init)

    def finish(i, carry):
        rows = pl.ds(pl.multiple_of(i * TQ, TQ), TQ)
        r = 1.0 / acc_ref[i, DIFF_VDIM:DIFF_VDIM + 1, :]
        acc = acc_ref[i, 0:DIFF_VDIM, :]
        ot = acc[:, 0:TQ] * r[:, 0:TQ] - lam * (acc[:, TQ:] * r[:, TQ:])
        ms = jnp.mean(ot * ot, axis=0, keepdims=True)
        ot = ot * lax.rsqrt(ms + SUBLN_EPS) * sub
        o_ref[rows, :] = (ot.T * sgb_ref[rows, :].astype(jnp.float32)).astype(o_ref.dtype)
        return carry

    lax.fori_loop(0, nq, finish, 0, unroll=2)


def _attn_call(q, k, vt, sgb, lq1, lk1, lq2, lk2, subln_col, lam_init):
    B, S, _ = q.shape
    nq = S // TQ
    qtab, ctab, n_full = _pair_tables(S)
    n_pairs = len(qtab)
    assert n_full >= 2
    zero = np.zeros((1,), np.int32)
    head = lambda b, h, *_: (b, 0, h)
    const2 = lambda b, h, *_: (0, 0)
    grid_spec = pltpu.PrefetchScalarGridSpec(
        num_scalar_prefetch=3,
        grid=(B, N_DIFF_HEADS),
        in_specs=[
            pl.BlockSpec((None, S, DIFF_VDIM), head),
            pl.BlockSpec((None, S, DIFF_VDIM), head),
            pl.BlockSpec((None, S // TK, DIFF_VDIM, TK), lambda b, h, *_: (b, 0, h, 0)),
            pl.BlockSpec((None, S, DIFF_VDIM), head),
            pl.BlockSpec((1, DIFF_QKDIM), const2),
            pl.BlockSpec((1, DIFF_QKDIM), const2),
            pl.BlockSpec((1, DIFF_QKDIM), const2),
            pl.BlockSpec((1, DIFF_QKDIM), const2),
            pl.BlockSpec((DIFF_VDIM, 1), const2),
        ],
        out_specs=pl.BlockSpec((None, S, DIFF_VDIM), head),
        scratch_shapes=[
            pltpu.VMEM((nq, 2 * TQ, DIFF_VDIM), jnp.bfloat16),
            pltpu.VMEM((S // TK, DIFF_VDIM + VT_PAD, TK), jnp.bfloat16),
            pltpu.VMEM((nq, 1, 2 * TQ), jnp.float32),
            pltpu.VMEM((nq, DIFF_VDIM + VT_PAD, 2 * TQ), jnp.float32),
            pltpu.VMEM((PIPE_BUFS, 1, 2 * TQ), jnp.float32),
        ] + [pltpu.VMEM((TK, 2 * TQ), jnp.float32)] * PIPE_BUFS,
    )
    return pl.pallas_call(
        functools.partial(_attn_kernel, lam_init, n_full, n_pairs),
        out_shape=jax.ShapeDtypeStruct((B, S, ATTN_WIDTH), jnp.bfloat16),
        grid_spec=grid_spec,
        compiler_params=pltpu.CompilerParams(
            dimension_semantics=("arbitrary", "arbitrary"),
            vmem_limit_bytes=VMEM_LIMIT),
        name="l0_diff_attn",
    )(qtab, ctab, zero, q, k, vt, sgb, lq1, lk1, lq2, lk2, subln_col)


def _out0_kernel(ya_ref, yb_ref, x_ref, w_ref, g_ref, o_ref):
    out = _dot(ya_ref[...], w_ref[0:POOL_WIDTH, :]) + _dot(yb_ref[...], w_ref[POOL_WIDTH:, :])
    o_ref[...] = x_ref[...] + _rms_norm(out, g_ref[...], EPS)


def _out0_call(ya, yb, x, w_out, g, layer):
    n = x.shape[0]
    tm = TM_OUT0
    tok = lambda i: (i, 0)
    const2 = lambda i: (0, 0)
    return pl.pallas_call(
        _out0_kernel,
        out_shape=jax.ShapeDtypeStruct((n, D_MODEL), jnp.float32),
        grid=(n // tm,),
        in_specs=[
            pl.BlockSpec((tm, POOL_WIDTH), tok),
            pl.BlockSpec((tm, ATTN_WIDTH), tok),
            pl.BlockSpec((tm, D_MODEL), tok),
            pl.BlockSpec((None, D_INNER, D_MODEL), lambda i: (layer, 0, 0),
                         pipeline_mode=pl.Buffered(1)),
            pl.BlockSpec((1, D_MODEL), const2),
        ],
        out_specs=pl.BlockSpec((tm, D_MODEL), tok),
        compiler_params=pltpu.CompilerParams(
            dimension_semantics=("arbitrary",),
            vmem_limit_bytes=VMEM_LIMIT),
        name="l0_out_proj",
    )(ya, yb, x, w_out, g)


def _l1_kernel(zero_ref, x_ref, g_ref, win_ref, lng_ref, lnb_ref, ws_ref, bs_ref, wout_ref, pg_ref,
               o_ref, u_ref, v_ref, gt_ref, vn_ref, y_ref, *raw_refs):
    tm = x_ref.shape[0]
    x = x_ref[...]
    h = _rms_norm(x, g_ref[...], EPS).astype(jnp.bfloat16)
    staged_rows = pl.ds(pl.multiple_of(zero_ref[0], tm), tm)
    nblk = D_INNER // L1_COL_BLOCK
    dests = ([(v_ref, _gelu_tanh, D_INNER)] * nblk + [(u_ref, _gelu_tanh, 0)] * nblk
             + [(gt_ref, _silu, 2 * D_INNER)] * nblk)

    def activate(i):
        dest_ref, act, _ = dests[i]
        j = i % nblk
        raw = raw_refs[i % L1_RAW_BUFS][staged_rows, :]
        dest_ref[:, j * L1_COL_BLOCK:(j + 1) * L1_COL_BLOCK] = act(raw)

    def layer_norm_v():
        vf = v_ref[...]
        mu = jnp.mean(vf, axis=-1, keepdims=True)
        vc = vf - mu
        var = jnp.mean(vc * vc, axis=-1, keepdims=True)
        vn_ref[...] = (vc * lax.rsqrt(var + EPS) * lng_ref[...] + lnb_ref[...]).astype(vn_ref.dtype)

    for i, (_, _, base) in enumerate(dests):
        col = base + (i % nblk) * L1_COL_BLOCK
        raw_refs[i % L1_RAW_BUFS][...] = _dot(h, win_ref[:, col:col + L1_COL_BLOCK])
        if i > 0:
            activate(i - 1)
        if i == nblk:
            layer_norm_v()
    activate(len(dests) - 1)

    tri_r = lax.broadcasted_iota(jnp.int32, (SGU_CHUNK, SGU_CHUNK), 0)
    tri_c = lax.broadcasted_iota(jnp.int32, (SGU_CHUNK, SGU_CHUNK), 1)
    tril = tri_c <= tri_r
    for g in range(SGU_GROUPS):
        cols = slice(g * SGU_GROUP_DIM, (g + 1) * SGU_GROUP_DIM)
        wg = jnp.where(tril, ws_ref[g], 0.0).astype(jnp.bfloat16)
        bias = bs_ref[:, g:g + 1]
        for c in range(tm // SGU_CHUNK):
            rows = slice(c * SGU_CHUNK, (c + 1) * SGU_CHUNK)
            mixed = _dot(wg, vn_ref[rows, cols]) + bias
            y_ref[rows, cols] = (u_ref[rows, cols] * mixed * gt_ref[rows, cols]).astype(y_ref.dtype)

    out = _dot(y_ref[...], wout_ref[...])
    o_ref[...] = x + _rms_norm(out, pg_ref[...], EPS)


def _l1_call(x, g, w_in, ln_g, ln_b, w_s, b_s_t, w_out, post_g, layer):
    n = x.shape[0]
    tm = TM_L1
    tok = lambda i: (i, 0)
    const2 = lambda i: (0, 0)
    return pl.pallas_call(
        _l1_kernel,
        out_shape=jax.ShapeDtypeStruct((n, D_MODEL), jnp.float32),
        grid=(n // tm,),
        in_specs=[
            pl.BlockSpec(memory_space=pltpu.SMEM),
            pl.BlockSpec((tm, D_MODEL), tok),
            pl.BlockSpec((1, D_MODEL), const2),
            pl.BlockSpec((None, D_MODEL, IN_COLS), lambda i: (layer, 0, 0),
                         pipeline_mode=pl.Buffered(1)),
            pl.BlockSpec((1, D_INNER), const2),
            pl.BlockSpec((1, D_INNER), const2),
            pl.BlockSpec((SGU_GROUPS, SGU_CHUNK, SGU_CHUNK), lambda i: (0, 0, 0)),
            pl.BlockSpec((SGU_CHUNK, SGU_GROUPS), const2),
            pl.BlockSpec((None, D_INNER, D_MODEL), lambda i: (layer, 0, 0),
                         pipeline_mode=pl.Buffered(1)),
            pl.BlockSpec((1, D_MODEL), const2),
        ],
        out_specs=pl.BlockSpec((tm, D_MODEL), tok),
        scratch_shapes=[
            pltpu.VMEM((tm, D_INNER), jnp.float32),
            pltpu.VMEM((tm, D_INNER), jnp.float32),
            pltpu.VMEM((tm, D_INNER), jnp.float32),
            pltpu.VMEM((tm, D_INNER), jnp.bfloat16),
            pltpu.VMEM((tm, D_INNER), jnp.bfloat16),
        ] + [pltpu.VMEM((tm, L1_COL_BLOCK), jnp.float32)] * L1_RAW_BUFS,
        compiler_params=pltpu.CompilerParams(
            dimension_semantics=("arbitrary",),
            vmem_limit_bytes=VMEM_LIMIT),
        name="l1_sgu_layer",
    )(jnp.zeros((1,), jnp.int32), x, g, w_in, ln_g, ln_b, w_s, b_s_t, w_out, post_g)


def _rope_lane_tables():
    half = ROPE_DIM // 2
    inv_freq = jnp.power(ROPE_THETA, -jnp.arange(half, dtype=jnp.float32) * 2.0 / ROPE_DIM)
    d = jnp.arange(LANES) % DIFF_QKDIM
    invf = jnp.where(d < ROPE_DIM, inv_freq[d % half], 0.0).astype(jnp.float32)
    m1 = jnp.where(d < half, -1.0, 0.0).astype(jnp.float32)
    m2 = jnp.where((d >= half) & (d < ROPE_DIM), 1.0, 0.0).astype(jnp.float32)
    return invf[None, :], m1[None, :], m2[None, :]


def kernel(x, positions, pre_norm, post_norm, w_in, w_out, pool_w, pool_scale, lam_q1, lam_k1, lam_q2, lam_k2, diff_subln, sgu_ln_g, sgu_ln_b, sgu_w, sgu_b):
    B, S, D = x.shape
    bf16 = jnp.bfloat16
    w_in_b = w_in.astype(bf16)
    w_out_b = w_out.astype(bf16)
    invf, m1, m2 = _rope_lane_tables()
    pos_f = positions.astype(jnp.float32)[:, :, None]

    q, k, vt, ya, sgb = _in0_call(x, pos_f, pre_norm[0:1], w_in_b, pool_w[0].astype(bf16),
                                  pool_scale[0:1], invf, m1, m2, layer=0)
    lam_init = 0.8 - 0.6 * math.exp(-0.3 * 0)
    yb = _attn_call(q, k, vt, sgb, lam_q1[0:1], lam_k1[0:1], lam_q2[0:1], lam_k2[0:1],
                    diff_subln[0].reshape(DIFF_VDIM, 1), lam_init)
    n = B * S
    x1 = _out0_call(ya.reshape(n, POOL_WIDTH), yb.reshape(n, ATTN_WIDTH), x.reshape(n, D),
                    w_out_b, post_norm[0:1], layer=0)

    x2 = _l1_call(x1, pre_norm[1:2], w_in_b, sgu_ln_g[0:1], sgu_ln_b[0:1], sgu_w[0],
                  sgu_b[0].T, w_out_b, post_norm[1:2], layer=1)
    return x2.reshape(B, S, D)
```

```python
import functools
import math

import jax
import jax.numpy as jnp
import numpy as np
from jax import lax
from jax.experimental import pallas as pl
from jax.experimental.pallas import tpu as pltpu

D_MODEL = 1024
D_INNER = 2048
IN_COLS = 3 * D_INNER
POOL_WIDTH = 1024
POOL_GROUPS = 4
POOL_GROUP_DIM = 256
POOL_WINDOWS = (2, 4, 8, 16)
POOL_HALO = 16
ATTN_WIDTH = 1024
N_DIFF_HEADS = 8
DIFF_VDIM = 128
DIFF_QKDIM = 64
ROPE_THETA = 500000.0
ROPE_DIM = 16
SGU_CHUNK = 128
SGU_GROUPS = 8
SGU_GROUP_DIM = 256
EPS = 1e-6
SUBLN_EPS = 1e-5
LANES = 128
LOG2_E = math.log2(math.e)

TQ = 512
TK = 512
TM_IN0 = TK
VT_PAD = 16
PIPE_BUFS = 3
PIPE_UNROLL = 12
assert PIPE_UNROLL % PIPE_BUFS == 0
TM_OUT0 = 512
TM_L1 = 512
L1_COL_BLOCK = 512
L1_RAW_BUFS = 5
VMEM_LIMIT = 56 * 1024 * 1024

_NT = (((1,), (1,)), ((), ()))


def _rms_norm(x, g, eps):
    return x * lax.rsqrt(jnp.mean(x * x, axis=-1, keepdims=True) + eps) * g


def _silu(x):
    return x / (1.0 + jnp.exp(-x))


def _gelu_tanh(x):
    c = math.sqrt(2.0 / math.pi)
    return 0.5 * x * (1.0 + jnp.tanh(c * (x + 0.044715 * (x * x * x))))


def _dot(a, b):
    return jnp.dot(a, b, preferred_element_type=jnp.float32)


def _in0_kernel(x_ref, pos_ref, g_ref, w_ref, pw_ref, ps_ref, invf_ref, m1_ref, m2_ref,
                q_ref, k_ref, vt_ref, ya_ref, sgb_ref, aext_ref):
    i = pl.program_id(1)
    tm = x_ref.shape[0]

    @pl.when(i == 0)
    def _():
        aext_ref[0:POOL_HALO, :] = jnp.zeros((POOL_HALO, POOL_WIDTH), jnp.float32)

    @pl.when(i > 0)
    def _():
        aext_ref[0:POOL_HALO, :] = aext_ref[tm:tm + POOL_HALO, :]

    h = _rms_norm(x_ref[...], g_ref[...], EPS).astype(jnp.bfloat16)

    def proj(col):
        return _dot(h, w_ref[:, col * 1024:(col + 1) * 1024])

    vt_ref[...] = proj(3).T.astype(vt_ref.dtype)
    aext_ref[POOL_HALO:, :] = proj(0)
    gate_a = _silu(proj(4))
    t_idx = i * tm + lax.broadcasted_iota(jnp.int32, (tm, 1), 0)
    for g, w in enumerate(POOL_WINDOWS):
        cols = slice(g * POOL_GROUP_DIM, (g + 1) * POOL_GROUP_DIM)
        e = aext_ref[:, cols]
        acc = e
        span = 1
        while span < w:
            acc = acc + pltpu.roll(acc, span, axis=0)
            span *= 2
        cnt = jnp.minimum(t_idx + 1, w).astype(jnp.float32)
        pooled = acc[POOL_HALO:, :] / cnt - e[POOL_HALO:, :]
        y = _dot(pooled.astype(jnp.bfloat16), pw_ref[g]) * ps_ref[:, cols]
        ya_ref[:, cols] = (y * gate_a[:, cols]).astype(ya_ref.dtype)

    ang = pos_ref[...] * invf_ref[...]
    cs = jnp.cos(ang)
    sn = jnp.sin(ang)
    s_lo = sn * m1_ref[...]
    s_hi = sn * m2_ref[...]
    half = ROPE_DIM // 2

    def rope_store(t, out_ref, scale):
        for c in range(ATTN_WIDTH // LANES):
            tc = t[:, c * LANES:(c + 1) * LANES]
            up = pltpu.roll(tc, LANES - half, axis=1)
            dn = pltpu.roll(tc, half, axis=1)
            r = tc * cs + up * s_lo + dn * s_hi
            if scale != 1.0:
                r = r * scale
            out_ref[:, c * LANES:(c + 1) * LANES] = r.astype(out_ref.dtype)

    rope_store(proj(1), q_ref, DIFF_QKDIM ** -0.5 * LOG2_E)
    rope_store(proj(2), k_ref, 1.0)
    sgb_ref[...] = _silu(proj(5)).astype(sgb_ref.dtype)


def _in0_call(x, pos_f, g, w_in, pool_w, pool_scale, invf, m1, m2, layer):
    B, S, _ = x.shape
    tm = TM_IN0
    tok = lambda b, i: (b, i, 0)
    const2 = lambda b, i: (0, 0)
    out = jax.ShapeDtypeStruct((B, S, 1024), jnp.bfloat16)
    out_vt = jax.ShapeDtypeStruct((B, S // tm, ATTN_WIDTH, tm), jnp.bfloat16)
    spec = pl.BlockSpec((None, tm, 1024), tok)
    spec_vt = pl.BlockSpec((None, None, ATTN_WIDTH, tm), lambda b, i: (b, i, 0, 0))
    return pl.pallas_call(
        _in0_kernel,
        out_shape=[out, out, out_vt, out, out],
        grid=(B, S // tm),
        in_specs=[
            pl.BlockSpec((None, tm, D_MODEL), tok),
            pl.BlockSpec((None, tm, 1), tok),
            pl.BlockSpec((1, D_MODEL), const2),
            pl.BlockSpec((None, D_MODEL, IN_COLS), lambda b, i: (layer, 0, 0),
                         pipeline_mode=pl.Buffered(1)),
            pl.BlockSpec((POOL_GROUPS, POOL_GROUP_DIM, POOL_GROUP_DIM), lambda b, i: (0, 0, 0)),
            pl.BlockSpec((1, POOL_WIDTH), const2),
            pl.BlockSpec((1, LANES), const2),
            pl.BlockSpec((1, LANES), const2),
            pl.BlockSpec((1, LANES), const2),
        ],
        out_specs=[spec, spec, spec_vt, spec, spec],
        scratch_shapes=[pltpu.VMEM((POOL_HALO + tm, POOL_WIDTH), jnp.float32)],
        compiler_params=pltpu.CompilerParams(
            dimension_semantics=("arbitrary", "arbitrary"),
            vmem_limit_bytes=VMEM_LIMIT),
        name="l0_in_proj",
    )(x, pos_f, g, w_in, pool_w, pool_scale, invf, m1, m2)


def _pair_tables(seq):
    nq = seq // TQ
    n_blocks = [(qi * TQ + TQ - 1) // TK + 1 for qi in range(nq)]
    full = [(qi, c) for qi in range(nq) for c in range(n_blocks[qi] - 1)]
    diag = [(qi, n_blocks[qi] - 1) for qi in range(nq)]
    pairs = np.array(full + diag, np.int32)
    return pairs[:, 0], pairs[:, 1], len(full)


def _attn_kernel(lam_init, n_full, n_pairs, qtab_ref, ctab_ref, zero_ref,
                 q_ref, k_ref, vt_ref, sgb_ref, lq1_ref, lk1_ref, lq2_ref, lk2_ref, sub_ref,
                 o_ref, qs_ref, vte_ref, m_ref, acc_ref, cm_ref, *s_refs):
    staged_rows = pl.ds(pl.multiple_of(zero_ref[0], TK), TK)
    nq = q_ref.shape[0] // TQ
    lane = lax.broadcasted_iota(jnp.int32, (TQ, LANES), 1)

    def stack_queries(i, carry):
        q = q_ref[pl.ds(pl.multiple_of(i * TQ, TQ), TQ), :]
        zero = jnp.zeros_like(q)
        qs_ref[i, 0:TQ, :] = jnp.where(lane < DIFF_QKDIM, q, zero)
        qs_ref[i, TQ:, :] = jnp.where(lane >= DIFF_QKDIM, q, zero)
        return carry

    lax.fori_loop(0, nq, stack_queries, 0)
    for c in range(vt_ref.shape[0]):
        vte_ref[c, 0:DIFF_VDIM, :] = vt_ref[c]
        vte_ref[c, DIFF_VDIM:, :] = jnp.ones((VT_PAD, TK), vte_ref.dtype)
    m_ref[...] = jnp.full(m_ref.shape, -jnp.inf, jnp.float32)
    acc_ref[...] = jnp.zeros(acc_ref.shape, jnp.float32)

    def scores(t, buf, masked):
        qi = qtab_ref[t]
        start = pl.multiple_of(ctab_ref[t] * TK, TK)
        kc = k_ref[pl.ds(start, TK), :]
        s = lax.dot_general(kc, qs_ref[qi], _NT, preferred_element_type=jnp.float32)
        if masked:
            key = lax.broadcasted_iota(jnp.int32, (TK, 2 * TQ), 0)
            col = lax.broadcasted_iota(jnp.int32, (TK, 2 * TQ), 1)
            qpos = qi * TQ + jnp.where(col >= TQ, col - TQ, col)
            s = jnp.where(start + key <= qpos, s, -jnp.inf)
        s_refs[buf][...] = s
        cm_ref[buf] = jnp.max(s, axis=0, keepdims=True)

    def softmax(t, buf):
        qi = qtab_ref[t]
        m_old = m_ref[qi]
        m_new = jnp.maximum(m_old, cm_ref[buf])
        m_ref[qi] = m_new
        s = s_refs[buf][staged_rows, :]
        p = jnp.exp2((s - m_new).astype(jnp.bfloat16))
        acc_ref[qi] = jnp.exp2(m_old - m_new) * acc_ref[qi] + _dot(vte_ref[ctab_ref[t]], p)

    def stage(t, phase, masked):
        scores(t, phase % PIPE_BUFS, masked)
        softmax(t - 1, (phase - 1) % PIPE_BUFS)

    def run_stages(first, count, masked):
        trips, rest = divmod(count, PIPE_UNROLL)

        def body(j, carry):
            for u in range(PIPE_UNROLL):
                stage(first + PIPE_UNROLL * j + u, first + u, masked)
            return carry

        lax.fori_loop(0, trips, body, 0)
        for u in range(rest):
            stage(first + PIPE_UNROLL * trips + u, first + u, masked)

    scores(0, 0, False)
    run_stages(1, n_full - 1, False)
    run_stages(n_full, n_pairs - n_full, True)
    softmax(n_pairs - 1, (n_pairs - 1) % PIPE_BUFS)

    lam = (jnp.exp(jnp.sum(lq1_ref[...] * lk1_ref[...], axis=-1, keepdims=True))
           - jnp.exp(jnp.sum(lq2_ref[...] * lk2_ref[...], axis=-1, keepdims=True))
           + lam_init)
    sub = sub_ref[...] * (1.0 - lam_init)

    def finish(i, carry):
        rows = pl.ds(pl.multiple_of(i * TQ, TQ), TQ)
        r = 1.0 / acc_ref[i, DIFF_VDIM:DIFF_VDIM + 1, :]
        acc = acc_ref[i, 0:DIFF_VDIM, :]
        ot = acc[:, 0:TQ] * r[:, 0:TQ] - lam * (acc[:, TQ:] * r[:, TQ:])
        ms = jnp.mean(ot * ot, axis=0, keepdims=True)
        ot = ot * lax.rsqrt(ms + SUBLN_EPS) * sub
        o_ref[rows, :] = (ot.T * sgb_ref[rows, :].astype(jnp.float32)).astype(o_ref.dtype)
        return carry

    lax.fori_loop(0, nq, finish, 0, unroll=2)


def _attn_call(q, k, vt, sgb, lq1, lk1, lq2, lk2, subln_col, lam_init):
    B, S, _ = q.shape
    nq = S // TQ
    qtab, ctab, n_full = _pair_tables(S)
    n_pairs = len(qtab)
    assert n_full >= 2
    zero = np.zeros((1,), np.int32)
    head = lambda b, h, *_: (b, 0, h)
    const2 = lambda b, h, *_: (0, 0)
    grid_spec = pltpu.PrefetchScalarGridSpec(
        num_scalar_prefetch=3,
        grid=(B, N_DIFF_HEADS),
        in_specs=[
            pl.BlockSpec((None, S, DIFF_VDIM), head),
            pl.BlockSpec((None, S, DIFF_VDIM), head),
            pl.BlockSpec((None, S // TK, DIFF_VDIM, TK), lambda b, h, *_: (b, 0, h, 0)),
            pl.BlockSpec((None, S, DIFF_VDIM), head),
            pl.BlockSpec((1, DIFF_QKDIM), const2),
            pl.BlockSpec((1, DIFF_QKDIM), const2),
            pl.BlockSpec((1, DIFF_QKDIM), const2),
            pl.BlockSpec((1, DIFF_QKDIM), const2),
            pl.BlockSpec((DIFF_VDIM, 1), const2),
        ],
        out_specs=pl.BlockSpec((None, S, DIFF_VDIM), head),
        scratch_shapes=[
            pltpu.VMEM((nq, 2 * TQ, DIFF_VDIM), jnp.bfloat16),
            pltpu.VMEM((S // TK, DIFF_VDIM + VT_PAD, TK), jnp.bfloat16),
            pltpu.VMEM((nq, 1, 2 * TQ), jnp.float32),
            pltpu.VMEM((nq, DIFF_VDIM + VT_PAD, 2 * TQ), jnp.float32),
            pltpu.VMEM((PIPE_BUFS, 1, 2 * TQ), jnp.float32),
        ] + [pltpu.VMEM((TK, 2 * TQ), jnp.float32)] * PIPE_BUFS,
    )
    return pl.pallas_call(
        functools.partial(_attn_kernel, lam_init, n_full, n_pairs),
        out_shape=jax.ShapeDtypeStruct((B, S, ATTN_WIDTH), jnp.bfloat16),
        grid_spec=grid_spec,
        compiler_params=pltpu.CompilerParams(
            dimension_semantics=("arbitrary", "arbitrary"),
            vmem_limit_bytes=VMEM_LIMIT),
        name="l0_diff_attn",
    )(qtab, ctab, zero, q, k, vt, sgb, lq1, lk1, lq2, lk2, subln_col)


def _out0_kernel(ya_ref, yb_ref, x_ref, w_ref, g_ref, o_ref):
    out = _dot(ya_ref[...], w_ref[0:POOL_WIDTH, :]) + _dot(yb_ref[...], w_ref[POOL_WIDTH:, :])
    o_ref[...] = x_ref[...] + _rms_norm(out, g_ref[...], EPS)


def _out0_call(ya, yb, x, w_out, g, layer):
    n = x.shape[0]
    tm = TM_OUT0
    tok = lambda i: (i, 0)
    const2 = lambda i: (0, 0)
    return pl.pallas_call(
        _out0_kernel,
        out_shape=jax.ShapeDtypeStruct((n, D_MODEL), jnp.float32),
        grid=(n // tm,),
        in_specs=[
            pl.BlockSpec((tm, POOL_WIDTH), tok),
            pl.BlockSpec((tm, ATTN_WIDTH), tok),
            pl.BlockSpec((tm, D_MODEL), tok),
            pl.BlockSpec((None, D_INNER, D_MODEL), lambda i: (layer, 0, 0),
                         pipeline_mode=pl.Buffered(1)),
            pl.BlockSpec((1, D_MODEL), const2),
        ],
        out_specs=pl.BlockSpec((tm, D_MODEL), tok),
        compiler_params=pltpu.CompilerParams(
            dimension_semantics=("arbitrary",),
            vmem_limit_bytes=VMEM_LIMIT),
        name="l0_out_proj",
    )(ya, yb, x, w_out, g)


def _l1_kernel(zero_ref, ya_ref, yb_ref, x0_ref, w0_ref, pg0_ref,
               g_ref, win_ref, lng_ref, lnb_ref, ws_ref, bs_ref, wout_ref, pg_ref,
               o_ref, x1_ref, u_ref, v_ref, gt_ref, vn_ref, y_ref, *raw_refs):
    tm = x0_ref.shape[0]
    out0 = (_dot(ya_ref[...], w0_ref[0:POOL_WIDTH, :]) + _dot(yb_ref[...], w0_ref[POOL_WIDTH:, :]))
    x = x0_ref[...] + _rms_norm(out0, pg0_ref[...], EPS)
    x1_ref[...] = x
    h = _rms_norm(x, g_ref[...], EPS).astype(jnp.bfloat16)
    staged_rows = pl.ds(pl.multiple_of(zero_ref[0], tm), tm)
    nblk = D_INNER // L1_COL_BLOCK
    dests = ([(v_ref, _gelu_tanh, D_INNER)] * nblk + [(u_ref, _gelu_tanh, 0)] * nblk
             + [(gt_ref, _silu, 2 * D_INNER)] * nblk)

    def activate(i):
        dest_ref, act, _ = dests[i]
        j = i % nblk
        raw = raw_refs[i % L1_RAW_BUFS][staged_rows, :]
        dest_ref[:, j * L1_COL_BLOCK:(j + 1) * L1_COL_BLOCK] = act(raw)

    def layer_norm_v():
        vf = v_ref[...]
        mu = jnp.mean(vf, axis=-1, keepdims=True)
        vc = vf - mu
        var = jnp.mean(vc * vc, axis=-1, keepdims=True)
        vn_ref[...] = (vc * lax.rsqrt(var + EPS) * lng_ref[...] + lnb_ref[...]).astype(vn_ref.dtype)

    for i, (_, _, base) in enumerate(dests):
        col = base + (i % nblk) * L1_COL_BLOCK
        raw_refs[i % L1_RAW_BUFS][...] = _dot(h, win_ref[:, col:col + L1_COL_BLOCK])
        if i > 0:
            activate(i - 1)
        if i == nblk:
            layer_norm_v()
    activate(len(dests) - 1)

    tri_r = lax.broadcasted_iota(jnp.int32, (SGU_CHUNK, SGU_CHUNK), 0)
    tri_c = lax.broadcasted_iota(jnp.int32, (SGU_CHUNK, SGU_CHUNK), 1)
    tril = tri_c <= tri_r
    for g in range(SGU_GROUPS):
        cols = slice(g * SGU_GROUP_DIM, (g + 1) * SGU_GROUP_DIM)
        wg = jnp.where(tril, ws_ref[g], 0.0).astype(jnp.bfloat16)
        bias = bs_ref[:, g:g + 1]
        for c in range(tm // SGU_CHUNK):
            rows = slice(c * SGU_CHUNK, (c + 1) * SGU_CHUNK)
            mixed = _dot(wg, vn_ref[rows, cols]) + bias
            y_ref[rows, cols] = (u_ref[rows, cols] * mixed * gt_ref[rows, cols]).astype(y_ref.dtype)

    out = _dot(y_ref[...], wout_ref[...])
    o_ref[...] = x1_ref[...] + _rms_norm(out, pg_ref[...], EPS)


def _l1_call(ya, yb, x, w_out_prev, post_g_prev, g, w_in, ln_g, ln_b, w_s, b_s_t, w_out, post_g,
             layer):
    n = x.shape[0]
    tm = TM_L1
    tok = lambda i: (i, 0)
    const2 = lambda i: (0, 0)
    return pl.pallas_call(
        _l1_kernel,
        out_shape=jax.ShapeDtypeStruct((n, D_MODEL), jnp.float32),
        grid=(n // tm,),
        in_specs=[
            pl.BlockSpec(memory_space=pltpu.SMEM),
            pl.BlockSpec((tm, POOL_WIDTH), tok),
            pl.BlockSpec((tm, ATTN_WIDTH), tok),
            pl.BlockSpec((tm, D_MODEL), tok),
            pl.BlockSpec((None, D_INNER, D_MODEL), lambda i: (layer - 1, 0, 0),
                         pipeline_mode=pl.Buffered(1)),
            pl.BlockSpec((1, D_MODEL), const2),
            pl.BlockSpec((1, D_MODEL), const2),
            pl.BlockSpec((None, D_MODEL, IN_COLS), lambda i: (layer, 0, 0),
                         pipeline_mode=pl.Buffered(1)),
            pl.BlockSpec((1, D_INNER), const2),
            pl.BlockSpec((1, D_INNER), const2),
            pl.BlockSpec((SGU_GROUPS, SGU_CHUNK, SGU_CHUNK), lambda i: (0, 0, 0)),
            pl.BlockSpec((SGU_CHUNK, SGU_GROUPS), const2),
            pl.BlockSpec((None, D_INNER, D_MODEL), lambda i: (layer, 0, 0),
                         pipeline_mode=pl.Buffered(1)),
            pl.BlockSpec((1, D_MODEL), const2),
        ],
        out_specs=pl.BlockSpec((tm, D_MODEL), tok),
        scratch_shapes=[
            pltpu.VMEM((tm, D_MODEL), jnp.float32),
            pltpu.VMEM((tm, D_INNER), jnp.float32),
            pltpu.VMEM((tm, D_INNER), jnp.float32),
            pltpu.VMEM((tm, D_INNER), jnp.float32),
            pltpu.VMEM((tm, D_INNER), jnp.bfloat16),
            pltpu.VMEM((tm, D_INNER), jnp.bfloat16),
        ] + [pltpu.VMEM((tm, L1_COL_BLOCK), jnp.float32)] * L1_RAW_BUFS,
        compiler_params=pltpu.CompilerParams(
            dimension_semantics=("arbitrary",),
            vmem_limit_bytes=VMEM_LIMIT),
        name="l1_sgu_layer",
    )(jnp.zeros((1,), jnp.int32), ya, yb, x, w_out_prev, post_g_prev, g, w_in, ln_g, ln_b, w_s,
      b_s_t, w_out, post_g)


def _rope_lane_tables():
    half = ROPE_DIM // 2
    inv_freq = jnp.power(ROPE_THETA, -jnp.arange(half, dtype=jnp.float32) * 2.0 / ROPE_DIM)
    d = jnp.arange(LANES) % DIFF_QKDIM
    invf = jnp.where(d < ROPE_DIM, inv_freq[d % half], 0.0).astype(jnp.float32)
    m1 = jnp.where(d < half, -1.0, 0.0).astype(jnp.float32)
    m2 = jnp.where((d >= half) & (d < ROPE_DIM), 1.0, 0.0).astype(jnp.float32)
    return invf[None, :], m1[None, :], m2[None, :]


def kernel(x, positions, pre_norm, post_norm, w_in, w_out, pool_w, pool_scale, lam_q1, lam_k1, lam_q2, lam_k2, diff_subln, sgu_ln_g, sgu_ln_b, sgu_w, sgu_b):
    B, S, D = x.shape
    bf16 = jnp.bfloat16
    w_in_b = w_in.astype(bf16)
    w_out_b = w_out.astype(bf16)
    invf, m1, m2 = _rope_lane_tables()
    pos_f = positions.astype(jnp.float32)[:, :, None]

    q, k, vt, ya, sgb = _in0_call(x, pos_f, pre_norm[0:1], w_in_b, pool_w[0].astype(bf16),
                                  pool_scale[0:1], invf, m1, m2, layer=0)
    lam_init = 0.8 - 0.6 * math.exp(-0.3 * 0)
    yb = _attn_call(q, k, vt, sgb, lam_q1[0:1], lam_k1[0:1], lam_q2[0:1], lam_k2[0:1],
                    diff_subln[0].reshape(DIFF_VDIM, 1), lam_init)
    n = B * S

    x2 = _l1_call(ya.reshape(n, POOL_WIDTH), yb.reshape(n, ATTN_WIDTH), x.reshape(n, D),
                  w_out_b, post_norm[0:1], pre_norm[1:2], w_in_b, sgu_ln_g[0:1], sgu_ln_b[0:1],
                  sgu_w[0], sgu_b[0].T, w_out_b, post_norm[1:2], layer=1)
    return x2.reshape(B, S, D)
```

```python
import functools
import math

import jax
import jax.numpy as jnp
import numpy as np
from jax import lax
from jax.experimental import pallas as pl
from jax.experimental.pallas import tpu as pltpu

D_MODEL = 1024
D_INNER = 2048
IN_COLS = 3 * D_INNER
POOL_WIDTH = 1024
POOL_GROUPS = 4
POOL_GROUP_DIM = 256
POOL_WINDOWS = (2, 4, 8, 16)
POOL_HALO = 16
ATTN_WIDTH = 1024
N_DIFF_HEADS = 8
DIFF_VDIM = 128
DIFF_QKDIM = 64
ROPE_THETA = 500000.0
ROPE_DIM = 16
SGU_CHUNK = 128
SGU_GROUPS = 8
SGU_GROUP_DIM = 256
EPS = 1e-6
SUBLN_EPS = 1e-5
LANES = 128
LOG2_E = math.log2(math.e)

TQ = 512
TK = 512
TM_IN0 = TK
VT_PAD = 16
PIPE_BUFS = 4
PIPE_UNROLL = 12
assert PIPE_UNROLL % PIPE_BUFS == 0
TM_L1 = 512
L1_COL_BLOCK = 512
L1_RAW_BUFS = 6
VMEM_LIMIT = 56 * 1024 * 1024

_NT = (((1,), (1,)), ((), ()))


def _rms_norm(x, g, eps):
    return x * lax.rsqrt(jnp.mean(x * x, axis=-1, keepdims=True) + eps) * g


def _silu(x):
    return x / (1.0 + jnp.exp(-x))


def _gelu_tanh(x):
    c = math.sqrt(2.0 / math.pi)
    return 0.5 * x * (1.0 + jnp.tanh(c * (x + 0.044715 * (x * x * x))))


def _dot(a, b):
    return jnp.dot(a, b, preferred_element_type=jnp.float32)


def _in0_kernel(x_ref, pos_ref, g_ref, w_ref, pw_ref, ps_ref, invf_ref, m1_ref, m2_ref,
                q_ref, k_ref, vt_ref, ya_ref, sgb_ref, aext_ref):
    i = pl.program_id(1)
    tm = x_ref.shape[0]

    @pl.when(i == 0)
    def _():
        aext_ref[0:POOL_HALO, :] = jnp.zeros((POOL_HALO, POOL_WIDTH), jnp.float32)

    @pl.when(i > 0)
    def _():
        aext_ref[0:POOL_HALO, :] = aext_ref[tm:tm + POOL_HALO, :]

    h = _rms_norm(x_ref[...], g_ref[...], EPS).astype(jnp.bfloat16)

    def proj(col):
        return _dot(h, w_ref[:, col * 1024:(col + 1) * 1024])

    vt_ref[...] = proj(3).T.astype(vt_ref.dtype)
    aext_ref[POOL_HALO:, :] = proj(0)
    gate_a = _silu(proj(4))
    t_idx = i * tm + lax.broadcasted_iota(jnp.int32, (tm, 1), 0)
    for g, w in enumerate(POOL_WINDOWS):
        cols = slice(g * POOL_GROUP_DIM, (g + 1) * POOL_GROUP_DIM)
        e = aext_ref[:, cols]
        acc = e
        span = 1
        while span < w:
            acc = acc + pltpu.roll(acc, span, axis=0)
            span *= 2
        cnt = jnp.minimum(t_idx + 1, w).astype(jnp.float32)
        pooled = acc[POOL_HALO:, :] / cnt - e[POOL_HALO:, :]
        y = _dot(pooled.astype(jnp.bfloat16), pw_ref[g]) * ps_ref[:, cols]
        ya_ref[:, cols] = (y * gate_a[:, cols]).astype(ya_ref.dtype)

    ang = pos_ref[...] * invf_ref[...]
    cs = jnp.cos(ang)
    sn = jnp.sin(ang)
    s_lo = sn * m1_ref[...]
    s_hi = sn * m2_ref[...]
    half = ROPE_DIM // 2

    def rope_store(t, out_ref, scale):
        for c in range(ATTN_WIDTH // LANES):
            tc = t[:, c * LANES:(c + 1) * LANES]
            up = pltpu.roll(tc, LANES - half, axis=1)
            dn = pltpu.roll(tc, half, axis=1)
            r = tc * cs + up * s_lo + dn * s_hi
            if scale != 1.0:
                r = r * scale
            out_ref[:, c * LANES:(c + 1) * LANES] = r.astype(out_ref.dtype)

    rope_store(proj(1), q_ref, DIFF_QKDIM ** -0.5 * LOG2_E)
    rope_store(proj(2), k_ref, 1.0)
    sgb_ref[...] = _silu(proj(5)).astype(sgb_ref.dtype)


def _in0_call(x, pos_f, g, w_in, pool_w, pool_scale, invf, m1, m2, layer):
    B, S, _ = x.shape
    tm = TM_IN0
    tok = lambda b, i: (b, i, 0)
    const2 = lambda b, i: (0, 0)
    out = jax.ShapeDtypeStruct((B, S, 1024), jnp.bfloat16)
    out_vt = jax.ShapeDtypeStruct((B, S // tm, ATTN_WIDTH, tm), jnp.bfloat16)
    spec = pl.BlockSpec((None, tm, 1024), tok)
    spec_vt = pl.BlockSpec((None, None, ATTN_WIDTH, tm), lambda b, i: (b, i, 0, 0))
    return pl.pallas_call(
        _in0_kernel,
        out_shape=[out, out, out_vt, out, out],
        grid=(B, S // tm),
        in_specs=[
            pl.BlockSpec((None, tm, D_MODEL), tok),
            pl.BlockSpec((None, tm, 1), tok),
            pl.BlockSpec((1, D_MODEL), const2),
            pl.BlockSpec((None, D_MODEL, IN_COLS), lambda b, i: (layer, 0, 0),
                         pipeline_mode=pl.Buffered(1)),
            pl.BlockSpec((POOL_GROUPS, POOL_GROUP_DIM, POOL_GROUP_DIM), lambda b, i: (0, 0, 0)),
            pl.BlockSpec((1, POOL_WIDTH), const2),
            pl.BlockSpec((1, LANES), const2),
            pl.BlockSpec((1, LANES), const2),
            pl.BlockSpec((1, LANES), const2),
        ],
        out_specs=[spec, spec, spec_vt, spec, spec],
        scratch_shapes=[pltpu.VMEM((POOL_HALO + tm, POOL_WIDTH), jnp.float32)],
        compiler_params=pltpu.CompilerParams(
            dimension_semantics=("arbitrary", "arbitrary"),
            vmem_limit_bytes=VMEM_LIMIT),
        name="l0_in_proj",
    )(x, pos_f, g, w_in, pool_w, pool_scale, invf, m1, m2)


def _pair_tables(seq):
    nq = seq // TQ
    n_blocks = [(qi * TQ + TQ - 1) // TK + 1 for qi in range(nq)]
    full = [(qi, c) for qi in range(nq) for c in range(n_blocks[qi] - 1)]
    diag = [(qi, n_blocks[qi] - 1) for qi in range(nq)]
    pairs = np.array(full + diag, np.int32)
    return pairs[:, 0], pairs[:, 1], len(full)


def _attn_kernel(lam_init, n_full, n_pairs, qtab_ref, ctab_ref, zero_ref,
                 q_ref, k_ref, vt_ref, sgb_ref, lq1_ref, lk1_ref, lq2_ref, lk2_ref, sub_ref,
                 o_ref, qs_ref, vte_ref, m_ref, acc_ref, cm_ref, *s_refs):
    staged_rows = pl.ds(pl.multiple_of(zero_ref[0], TK), TK)
    nq = q_ref.shape[0] // TQ
    lane = lax.broadcasted_iota(jnp.int32, (TQ, LANES), 1)

    def stack_queries(i, carry):
        q = q_ref[pl.ds(pl.multiple_of(i * TQ, TQ), TQ), :]
        zero = jnp.zeros_like(q)
        qs_ref[i, 0:TQ, :] = jnp.where(lane < DIFF_QKDIM, q, zero)
        qs_ref[i, TQ:, :] = jnp.where(lane >= DIFF_QKDIM, q, zero)
        return carry

    lax.fori_loop(0, nq, stack_queries, 0)
    for c in range(vt_ref.shape[0]):
        vte_ref[c, 0:DIFF_VDIM, :] = vt_ref[c]
        vte_ref[c, DIFF_VDIM:, :] = jnp.ones((VT_PAD, TK), vte_ref.dtype)
    m_ref[...] = jnp.full(m_ref.shape, -jnp.inf, jnp.float32)
    acc_ref[...] = jnp.zeros(acc_ref.shape, jnp.float32)

    def scores(t, buf, masked):
        qi = qtab_ref[t]
        start = pl.multiple_of(ctab_ref[t] * TK, TK)
        kc = k_ref[pl.ds(start, TK), :]
        s = lax.dot_general(kc, qs_ref[qi], _NT, preferred_element_type=jnp.float32)
        if masked:
            key = lax.broadcasted_iota(jnp.int32, (TK, 2 * TQ), 0)
            col = lax.broadcasted_iota(jnp.int32, (TK, 2 * TQ), 1)
            qpos = qi * TQ + jnp.where(col >= TQ, col - TQ, col)
            s = jnp.where(start + key <= qpos, s, -jnp.inf)
        s_refs[buf][...] = s
        cm_ref[buf] = jnp.max(s, axis=0, keepdims=True)

    def softmax(t, buf):
        qi = qtab_ref[t]
        m_old = m_ref[qi]
        m_new = jnp.maximum(m_old, cm_ref[buf])
        m_ref[qi] = m_new
        s = s_refs[buf][staged_rows, :]
        p = jnp.exp2((s - m_new).astype(jnp.bfloat16))
        acc_ref[qi] = jnp.exp2(m_old - m_new) * acc_ref[qi] + _dot(vte_ref[ctab_ref[t]], p)

    def stage(t, phase, masked):
        scores(t, phase % PIPE_BUFS, masked)
        softmax(t - 1, (phase - 1) % PIPE_BUFS)

    def run_stages(first, count, masked):
        trips, rest = divmod(count, PIPE_UNROLL)

        def body(j, carry):
            for u in range(PIPE_UNROLL):
                stage(first + PIPE_UNROLL * j + u, first + u, masked)
            return carry

        lax.fori_loop(0, trips, body, 0)
        for u in range(rest):
            stage(first + PIPE_UNROLL * trips + u, first + u, masked)

    scores(0, 0, False)
    run_stages(1, n_full - 1, False)
    run_stages(n_full, n_pairs - n_full, True)
    softmax(n_pairs - 1, (n_pairs - 1) % PIPE_BUFS)

    lam = (jnp.exp(jnp.sum(lq1_ref[...] * lk1_ref[...], axis=-1, keepdims=True))
           - jnp.exp(jnp.sum(lq2_ref[...] * lk2_ref[...], axis=-1, keepdims=True))
           + lam_init)
    sub = sub_ref[...] * (1.0 - lam_init)

    def finish(i, carry):
        rows = pl.ds(pl.multiple_of(i * TQ, TQ), TQ)
        r = 1.0 / acc_ref[i, DIFF_VDIM:DIFF_VDIM + 1, :]
        acc = acc_ref[i, 0:DIFF_VDIM, :]
        ot = acc[:, 0:TQ] * r[:, 0:TQ] - lam * (acc[:, TQ:] * r[:, TQ:])
        ms = jnp.mean(ot * ot, axis=0, keepdims=True)
        ot = ot * lax.rsqrt(ms + SUBLN_EPS) * sub
        o_ref[rows, :] = (ot.T * sgb_ref[rows, :].astype(jnp.float32)).astype(o_ref.dtype)
        return carry

    lax.fori_loop(0, nq, finish, 0, unroll=2)


def _attn_call(q, k, vt, sgb, lq1, lk1, lq2, lk2, subln_col, lam_init):
    B, S, _ = q.shape
    nq = S // TQ
    qtab, ctab, n_full = _pair_tables(S)
    n_pairs = len(qtab)
    assert n_full >= 2
    zero = np.zeros((1,), np.int32)
    head = lambda b, h, *_: (b, 0, h)
    const2 = lambda b, h, *_: (0, 0)
    grid_spec = pltpu.PrefetchScalarGridSpec(
        num_scalar_prefetch=3,
        grid=(B, N_DIFF_HEADS),
        in_specs=[
            pl.BlockSpec((None, S, DIFF_VDIM), head),
            pl.BlockSpec((None, S, DIFF_VDIM), head),
            pl.BlockSpec((None, S // TK, DIFF_VDIM, TK), lambda b, h, *_: (b, 0, h, 0)),
            pl.BlockSpec((None, S, DIFF_VDIM), head),
            pl.BlockSpec((1, DIFF_QKDIM), const2),
            pl.BlockSpec((1, DIFF_QKDIM), const2),
            pl.BlockSpec((1, DIFF_QKDIM), const2),
            pl.BlockSpec((1, DIFF_QKDIM), const2),
            pl.BlockSpec((DIFF_VDIM, 1), const2),
        ],
        out_specs=pl.BlockSpec((None, S, DIFF_VDIM), head),
        scratch_shapes=[
            pltpu.VMEM((nq, 2 * TQ, DIFF_VDIM), jnp.bfloat16),
            pltpu.VMEM((S // TK, DIFF_VDIM + VT_PAD, TK), jnp.bfloat16),
            pltpu.VMEM((nq, 1, 2 * TQ), jnp.float32),
            pltpu.VMEM((nq, DIFF_VDIM + VT_PAD, 2 * TQ), jnp.float32),
            pltpu.VMEM((PIPE_BUFS, 1, 2 * TQ), jnp.float32),
        ] + [pltpu.VMEM((TK, 2 * TQ), jnp.float32)] * PIPE_BUFS,
    )
    return pl.pallas_call(
        functools.partial(_attn_kernel, lam_init, n_full, n_pairs),
        out_shape=jax.ShapeDtypeStruct((B, S, ATTN_WIDTH), jnp.bfloat16),
        grid_spec=grid_spec,
        compiler_params=pltpu.CompilerParams(
            dimension_semantics=("arbitrary", "arbitrary"),
            vmem_limit_bytes=VMEM_LIMIT),
        name="l0_diff_attn",
    )(qtab, ctab, zero, q, k, vt, sgb, lq1, lk1, lq2, lk2, subln_col)


def _l1_kernel(zero_ref, ya_ref, yb_ref, x0_ref, w0_ref, pg0_ref,
               g_ref, win_ref, lng_ref, lnb_ref, ws_ref, bs_ref, wout_ref, pg_ref,
               o_ref, x1_ref, u_ref, v_ref, gt_ref, vn_ref, y_ref, *raw_refs):
    tm = x0_ref.shape[0]
    out0 = (_dot(ya_ref[...], w0_ref[0:POOL_WIDTH, :]) + _dot(yb_ref[...], w0_ref[POOL_WIDTH:, :]))
    x = x0_ref[...] + _rms_norm(out0, pg0_ref[...], EPS)
    x1_ref[...] = x
    h = _rms_norm(x, g_ref[...], EPS).astype(jnp.bfloat16)
    staged_rows = pl.ds(pl.multiple_of(zero_ref[0], tm), tm)
    nblk = D_INNER // L1_COL_BLOCK
    dests = ([(v_ref, _gelu_tanh, D_INNER)] * nblk + [(u_ref, _gelu_tanh, 0)] * nblk
             + [(gt_ref, _silu, 2 * D_INNER)] * nblk)

    def activate(i):
        dest_ref, act, _ = dests[i]
        j = i % nblk
        raw = raw_refs[i % L1_RAW_BUFS][staged_rows, :]
        dest_ref[:, j * L1_COL_BLOCK:(j + 1) * L1_COL_BLOCK] = act(raw)

    def layer_norm_v():
        vf = v_ref[...]
        mu = jnp.mean(vf, axis=-1, keepdims=True)
        vc = vf - mu
        var = jnp.mean(vc * vc, axis=-1, keepdims=True)
        vn_ref[...] = (vc * lax.rsqrt(var + EPS) * lng_ref[...] + lnb_ref[...]).astype(vn_ref.dtype)

    for i, (_, _, base) in enumerate(dests):
        col = base + (i % nblk) * L1_COL_BLOCK
        raw_refs[i % L1_RAW_BUFS][...] = _dot(h, win_ref[:, col:col + L1_COL_BLOCK])
        if i > 0:
            activate(i - 1)
        if i == nblk:
            layer_norm_v()
    activate(len(dests) - 1)

    tri_r = lax.broadcasted_iota(jnp.int32, (SGU_CHUNK, SGU_CHUNK), 0)
    tri_c = lax.broadcasted_iota(jnp.int32, (SGU_CHUNK, SGU_CHUNK), 1)
    tril = tri_c <= tri_r
    for g in range(SGU_GROUPS):
        cols = slice(g * SGU_GROUP_DIM, (g + 1) * SGU_GROUP_DIM)
        wg = jnp.where(tril, ws_ref[g], 0.0).astype(jnp.bfloat16)
        bias = bs_ref[:, g:g + 1]
        for c in range(tm // SGU_CHUNK):
            rows = slice(c * SGU_CHUNK, (c + 1) * SGU_CHUNK)
            mixed = _dot(wg, vn_ref[rows, cols]) + bias
            y_ref[rows, cols] = (u_ref[rows, cols] * mixed * gt_ref[rows, cols]).astype(y_ref.dtype)

    out = _dot(y_ref[...], wout_ref[...])
    o_ref[...] = x1_ref[...] + _rms_norm(out, pg_ref[...], EPS)


def _l1_call(ya, yb, x, w_out_prev, post_g_prev, g, w_in, ln_g, ln_b, w_s, b_s_t, w_out, post_g,
             layer):
    n = x.shape[0]
    tm = TM_L1
    tok = lambda i: (i, 0)
    const2 = lambda i: (0, 0)
    return pl.pallas_call(
        _l1_kernel,
        out_shape=jax.ShapeDtypeStruct((n, D_MODEL), jnp.float32),
        grid=(n // tm,),
        in_specs=[
            pl.BlockSpec(memory_space=pltpu.SMEM),
            pl.BlockSpec((tm, POOL_WIDTH), tok),
            pl.BlockSpec((tm, ATTN_WIDTH), tok),
            pl.BlockSpec((tm, D_MODEL), tok),
            pl.BlockSpec((None, D_INNER, D_MODEL), lambda i: (layer - 1, 0, 0),
                         pipeline_mode=pl.Buffered(1)),
            pl.BlockSpec((1, D_MODEL), const2),
            pl.BlockSpec((1, D_MODEL), const2),
            pl.BlockSpec((None, D_MODEL, IN_COLS), lambda i: (layer, 0, 0),
                         pipeline_mode=pl.Buffered(1)),
            pl.BlockSpec((1, D_INNER), const2),
            pl.BlockSpec((1, D_INNER), const2),
            pl.BlockSpec((SGU_GROUPS, SGU_CHUNK, SGU_CHUNK), lambda i: (0, 0, 0)),
            pl.BlockSpec((SGU_CHUNK, SGU_GROUPS), const2),
            pl.BlockSpec((None, D_INNER, D_MODEL), lambda i: (layer, 0, 0),
                         pipeline_mode=pl.Buffered(1)),
            pl.BlockSpec((1, D_MODEL), const2),
        ],
        out_specs=pl.BlockSpec((tm, D_MODEL), tok),
        scratch_shapes=[
            pltpu.VMEM((tm, D_MODEL), jnp.float32),
            pltpu.VMEM((tm, D_INNER), jnp.float32),
            pltpu.VMEM((tm, D_INNER), jnp.float32),
            pltpu.VMEM((tm, D_INNER), jnp.float32),
            pltpu.VMEM((tm, D_INNER), jnp.bfloat16),
            pltpu.VMEM((tm, D_INNER), jnp.bfloat16),
        ] + [pltpu.VMEM((tm, L1_COL_BLOCK), jnp.float32)] * L1_RAW_BUFS,
        compiler_params=pltpu.CompilerParams(
            dimension_semantics=("arbitrary",),
            vmem_limit_bytes=VMEM_LIMIT),
        name="l1_sgu_layer",
    )(jnp.zeros((1,), jnp.int32), ya, yb, x, w_out_prev, post_g_prev, g, w_in, ln_g, ln_b, w_s,
      b_s_t, w_out, post_g)


def _rope_lane_tables():
    half = ROPE_DIM // 2
    inv_freq = jnp.power(ROPE_THETA, -jnp.arange(half, dtype=jnp.float32) * 2.0 / ROPE_DIM)
    d = jnp.arange(LANES) % DIFF_QKDIM
    invf = jnp.where(d < ROPE_DIM, inv_freq[d % half], 0.0).astype(jnp.float32)
    m1 = jnp.where(d < half, -1.0, 0.0).astype(jnp.float32)
    m2 = jnp.where((d >= half) & (d < ROPE_DIM), 1.0, 0.0).astype(jnp.float32)
    return invf[None, :], m1[None, :], m2[None, :]


def kernel(x, positions, pre_norm, post_norm, w_in, w_out, pool_w, pool_scale, lam_q1, lam_k1, lam_q2, lam_k2, diff_subln, sgu_ln_g, sgu_ln_b, sgu_w, sgu_b):
    B, S, D = x.shape
    bf16 = jnp.bfloat16
    w_in_b = w_in.astype(bf16)
    w_out_b = w_out.astype(bf16)
    invf, m1, m2 = _rope_lane_tables()
    pos_f = positions.astype(jnp.float32)[:, :, None]

    q, k, vt, ya, sgb = _in0_call(x, pos_f, pre_norm[0:1], w_in_b, pool_w[0].astype(bf16),
                                  pool_scale[0:1], invf, m1, m2, layer=0)
    lam_init = 0.8 - 0.6 * math.exp(-0.3 * 0)
    yb = _attn_call(q, k, vt, sgb, lam_q1[0:1], lam_k1[0:1], lam_q2[0:1], lam_k2[0:1],
                    diff_subln[0].reshape(DIFF_VDIM, 1), lam_init)
    n = B * S

    x2 = _l1_call(ya.reshape(n, POOL_WIDTH), yb.reshape(n, ATTN_WIDTH), x.reshape(n, D),
                  w_out_b, post_norm[0:1], pre_norm[1:2], w_in_b, sgu_ln_g[0:1], sgu_ln_b[0:1],
                  sgu_w[0], sgu_b[0].T, w_out_b, post_norm[1:2], layer=1)
    return x2.reshape(B, S, D)
```

```python
import functools
import math

import jax
import jax.numpy as jnp
import numpy as np
from jax import lax
from jax.experimental import pallas as pl
from jax.experimental.pallas import tpu as pltpu

D_MODEL = 1024
D_INNER = 2048
IN_COLS = 3 * D_INNER
POOL_WIDTH = 1024
POOL_GROUPS = 4
POOL_GROUP_DIM = 256
POOL_WINDOWS = (2, 4, 8, 16)
POOL_HALO = 16
ATTN_WIDTH = 1024
N_DIFF_HEADS = 8
DIFF_VDIM = 128
DIFF_QKDIM = 64
ROPE_THETA = 500000.0
ROPE_DIM = 16
SGU_CHUNK = 128
SGU_GROUPS = 8
SGU_GROUP_DIM = 256
EPS = 1e-6
SUBLN_EPS = 1e-5
LANES = 128
LOG2_E = math.log2(math.e)

TQ = 512
TK = 512
TM_IN0 = TK
VT_PAD = 16
PIPE_BUFS = 4
PIPE_UNROLL = 12
assert PIPE_UNROLL % PIPE_BUFS == 0
TM_L1 = 512
L1_COL_BLOCK = 512
L1_RAW_BUFS = 6
L1_HEAD_SLABS = 4
VMEM_LIMIT = 56 * 1024 * 1024

_NT = (((1,), (1,)), ((), ()))


def _rms_norm(x, g, eps):
    return x * lax.rsqrt(jnp.mean(x * x, axis=-1, keepdims=True) + eps) * g


def _silu(x):
    return x / (1.0 + jnp.exp(-x))


def _gelu_tanh(x):
    c = math.sqrt(2.0 / math.pi)
    return 0.5 * x * (1.0 + jnp.tanh(c * (x + 0.044715 * (x * x * x))))


def _dot(a, b):
    return jnp.dot(a, b, preferred_element_type=jnp.float32)


def _in0_kernel(x_ref, pos_ref, g_ref, w_ref, pw_ref, ps_ref, invf_ref, m1_ref, m2_ref,
                q_ref, k_ref, vt_ref, ya_ref, sgb_ref, aext_ref):
    i = pl.program_id(1)
    tm = x_ref.shape[0]

    @pl.when(i == 0)
    def _():
        aext_ref[0:POOL_HALO, :] = jnp.zeros((POOL_HALO, POOL_WIDTH), jnp.float32)

    @pl.when(i > 0)
    def _():
        aext_ref[0:POOL_HALO, :] = aext_ref[tm:tm + POOL_HALO, :]

    h = _rms_norm(x_ref[...], g_ref[...], EPS).astype(jnp.bfloat16)

    def proj(col):
        return _dot(h, w_ref[:, col * 1024:(col + 1) * 1024])

    vt_ref[...] = proj(3).T.astype(vt_ref.dtype)
    aext_ref[POOL_HALO:, :] = proj(0)
    gate_a = _silu(proj(4))
    t_idx = i * tm + lax.broadcasted_iota(jnp.int32, (tm, 1), 0)
    for g, w in enumerate(POOL_WINDOWS):
        cols = slice(g * POOL_GROUP_DIM, (g + 1) * POOL_GROUP_DIM)
        e = aext_ref[:, cols]
        acc = e
        span = 1
        while span < w:
            acc = acc + pltpu.roll(acc, span, axis=0)
            span *= 2
        cnt = jnp.minimum(t_idx + 1, w).astype(jnp.float32)
        pooled = acc[POOL_HALO:, :] / cnt - e[POOL_HALO:, :]
        y = _dot(pooled.astype(jnp.bfloat16), pw_ref[g]) * ps_ref[:, cols]
        ya_ref[:, cols] = (y * gate_a[:, cols]).astype(ya_ref.dtype)

    ang = pos_ref[...].T * invf_ref[...]
    cs = jnp.cos(ang)
    sn = jnp.sin(ang)
    s_lo = sn * m1_ref[...]
    s_hi = sn * m2_ref[...]
    half = ROPE_DIM // 2

    def rope_store(t, out_ref, scale):
        for c in range(ATTN_WIDTH // LANES):
            tc = t[:, c * LANES:(c + 1) * LANES]
            up = pltpu.roll(tc, LANES - half, axis=1)
            dn = pltpu.roll(tc, half, axis=1)
            r = tc * cs + up * s_lo + dn * s_hi
            if scale != 1.0:
                r = r * scale
            out_ref[:, c * LANES:(c + 1) * LANES] = r.astype(out_ref.dtype)

    rope_store(proj(1), q_ref, DIFF_QKDIM ** -0.5 * LOG2_E)
    rope_store(proj(2), k_ref, 1.0)
    sgb_ref[...] = _silu(proj(5)).astype(sgb_ref.dtype)


def _in0_call(x, pos_f, g, w_in, pool_w, pool_scale, invf, m1, m2, layer):
    B, S, _ = x.shape
    tm = TM_IN0
    tok = lambda b, i: (b, i, 0)
    const2 = lambda b, i: (0, 0)
    out = jax.ShapeDtypeStruct((B, S, 1024), jnp.bfloat16)
    out_vt = jax.ShapeDtypeStruct((B, S // tm, ATTN_WIDTH, tm), jnp.bfloat16)
    spec = pl.BlockSpec((None, tm, 1024), tok)
    spec_vt = pl.BlockSpec((None, None, ATTN_WIDTH, tm), lambda b, i: (b, i, 0, 0))
    return pl.pallas_call(
        _in0_kernel,
        out_shape=[out, out, out_vt, out, out],
        grid=(B, S // tm),
        in_specs=[
            pl.BlockSpec((None, tm, D_MODEL), tok),
            pl.BlockSpec((None, None, 1, tm), lambda b, i: (b, i, 0, 0)),
            pl.BlockSpec((1, D_MODEL), const2),
            pl.BlockSpec((None, D_MODEL, IN_COLS), lambda b, i: (layer, 0, 0),
                         pipeline_mode=pl.Buffered(1)),
            pl.BlockSpec((POOL_GROUPS, POOL_GROUP_DIM, POOL_GROUP_DIM), lambda b, i: (0, 0, 0)),
            pl.BlockSpec((1, POOL_WIDTH), const2),
            pl.BlockSpec((1, LANES), const2),
            pl.BlockSpec((1, LANES), const2),
            pl.BlockSpec((1, LANES), const2),
        ],
        out_specs=[spec, spec, spec_vt, spec, spec],
        scratch_shapes=[pltpu.VMEM((POOL_HALO + tm, POOL_WIDTH), jnp.float32)],
        compiler_params=pltpu.CompilerParams(
            dimension_semantics=("arbitrary", "arbitrary"),
            vmem_limit_bytes=VMEM_LIMIT),
        name="l0_in_proj",
    )(x, pos_f, g, w_in, pool_w, pool_scale, invf, m1, m2)


def _pair_tables(seq):
    nq = seq // TQ
    n_blocks = [(qi * TQ + TQ - 1) // TK + 1 for qi in range(nq)]
    full = [(qi, c) for qi in range(nq) for c in range(n_blocks[qi] - 1)]
    diag = [(qi, n_blocks[qi] - 1) for qi in range(nq)]
    pairs = np.array(full + diag, np.int32)
    return pairs[:, 0], pairs[:, 1], len(full)


def _attn_kernel(lam_init, n_full, n_pairs, qtab_ref, ctab_ref, zero_ref,
                 q_ref, k_ref, vt_ref, sgb_ref, lq1_ref, lk1_ref, lq2_ref, lk2_ref, sub_ref,
                 o_ref, qs_ref, vte_ref, m_ref, acc_ref, cm_ref, *s_refs):
    staged_rows = pl.ds(pl.multiple_of(zero_ref[0], TK), TK)
    nq = q_ref.shape[0] // TQ
    lane = lax.broadcasted_iota(jnp.int32, (TQ, LANES), 1)

    def stack_queries(i, carry):
        q = q_ref[pl.ds(pl.multiple_of(i * TQ, TQ), TQ), :]
        zero = jnp.zeros_like(q)
        qs_ref[i, 0:TQ, :] = jnp.where(lane < DIFF_QKDIM, q, zero)
        qs_ref[i, TQ:, :] = jnp.where(lane >= DIFF_QKDIM, q, zero)
        return carry

    lax.fori_loop(0, nq, stack_queries, 0)
    for c in range(vt_ref.shape[0]):
        vte_ref[c, 0:DIFF_VDIM, :] = vt_ref[c]
        vte_ref[c, DIFF_VDIM:, :] = jnp.ones((VT_PAD, TK), vte_ref.dtype)
    m_ref[...] = jnp.full(m_ref.shape, -jnp.inf, jnp.float32)
    acc_ref[...] = jnp.zeros(acc_ref.shape, jnp.float32)

    def scores(t, buf, masked):
        qi = qtab_ref[t]
        start = pl.multiple_of(ctab_ref[t] * TK, TK)
        kc = k_ref[pl.ds(start, TK), :]
        s = lax.dot_general(kc, qs_ref[qi], _NT, preferred_element_type=jnp.float32)
        if masked:
            key = lax.broadcasted_iota(jnp.int32, (TK, 2 * TQ), 0)
            col = lax.broadcasted_iota(jnp.int32, (TK, 2 * TQ), 1)
            qpos = qi * TQ + jnp.where(col >= TQ, col - TQ, col)
            s = jnp.where(start + key <= qpos, s, -jnp.inf)
        s_refs[buf][...] = s
        cm_ref[buf] = jnp.max(s, axis=0, keepdims=True)

    def softmax(t, buf):
        qi = qtab_ref[t]
        m_old = m_ref[qi]
        m_new = jnp.maximum(m_old, cm_ref[buf])
        m_ref[qi] = m_new
        s = s_refs[buf][staged_rows, :]
        p = jnp.exp2((s - m_new).astype(jnp.bfloat16))
        acc_ref[qi] = jnp.exp2(m_old - m_new) * acc_ref[qi] + _dot(vte_ref[ctab_ref[t]], p)

    def stage(t, phase, masked):
        scores(t, phase % PIPE_BUFS, masked)
        softmax(t - 1, (phase - 1) % PIPE_BUFS)

    def run_stages(first, count, masked):
        trips, rest = divmod(count, PIPE_UNROLL)

        def body(j, carry):
            for u in range(PIPE_UNROLL):
                stage(first + PIPE_UNROLL * j + u, first + u, masked)
            return carry

        lax.fori_loop(0, trips, body, 0)
        for u in range(rest):
            stage(first + PIPE_UNROLL * trips + u, first + u, masked)

    scores(0, 0, False)
    run_stages(1, n_full - 1, False)
    run_stages(n_full, n_pairs - n_full, True)
    softmax(n_pairs - 1, (n_pairs - 1) % PIPE_BUFS)

    lam = (jnp.exp(jnp.sum(lq1_ref[...] * lk1_ref[...], axis=-1, keepdims=True))
           - jnp.exp(jnp.sum(lq2_ref[...] * lk2_ref[...], axis=-1, keepdims=True))
           + lam_init)
    sub = sub_ref[...] * (1.0 - lam_init)

    def finish(i, carry):
        rows = pl.ds(pl.multiple_of(i * TQ, TQ), TQ)
        r = 1.0 / acc_ref[i, DIFF_VDIM:DIFF_VDIM + 1, :]
        acc = acc_ref[i, 0:DIFF_VDIM, :]
        ot = acc[:, 0:TQ] * r[:, 0:TQ] - lam * (acc[:, TQ:] * r[:, TQ:])
        ms = jnp.mean(ot * ot, axis=0, keepdims=True)
        ot = ot * lax.rsqrt(ms + SUBLN_EPS) * sub
        o_ref[rows, :] = (ot.T * sgb_ref[rows, :].astype(jnp.float32)).astype(o_ref.dtype)
        return carry

    lax.fori_loop(0, nq, finish, 0, unroll=2)


def _attn_call(q, k, vt, sgb, lq1, lk1, lq2, lk2, subln_col, lam_init):
    B, S, _ = q.shape
    nq = S // TQ
    qtab, ctab, n_full = _pair_tables(S)
    n_pairs = len(qtab)
    assert n_full >= 2
    zero = np.zeros((1,), np.int32)
    head = lambda b, h, *_: (b, 0, h)
    const2 = lambda b, h, *_: (0, 0)
    grid_spec = pltpu.PrefetchScalarGridSpec(
        num_scalar_prefetch=3,
        grid=(B, N_DIFF_HEADS),
        in_specs=[
            pl.BlockSpec((None, S, DIFF_VDIM), head),
            pl.BlockSpec((None, S, DIFF_VDIM), head),
            pl.BlockSpec((None, S // TK, DIFF_VDIM, TK), lambda b, h, *_: (b, 0, h, 0)),
            pl.BlockSpec((None, S, DIFF_VDIM), head),
            pl.BlockSpec((1, DIFF_QKDIM), const2),
            pl.BlockSpec((1, DIFF_QKDIM), const2),
            pl.BlockSpec((1, DIFF_QKDIM), const2),
            pl.BlockSpec((1, DIFF_QKDIM), const2),
            pl.BlockSpec((DIFF_VDIM, 1), const2),
        ],
        out_specs=pl.BlockSpec((None, S, DIFF_VDIM), head),
        scratch_shapes=[
            pltpu.VMEM((nq, 2 * TQ, DIFF_VDIM), jnp.bfloat16),
            pltpu.VMEM((S // TK, DIFF_VDIM + VT_PAD, TK), jnp.bfloat16),
            pltpu.VMEM((nq, 1, 2 * TQ), jnp.float32),
            pltpu.VMEM((nq, DIFF_VDIM + VT_PAD, 2 * TQ), jnp.float32),
            pltpu.VMEM((PIPE_BUFS, 1, 2 * TQ), jnp.float32),
        ] + [pltpu.VMEM((TK, 2 * TQ), jnp.float32)] * PIPE_BUFS,
    )
    return pl.pallas_call(
        functools.partial(_attn_kernel, lam_init, n_full, n_pairs),
        out_shape=jax.ShapeDtypeStruct((B, S, ATTN_WIDTH), jnp.bfloat16),
        grid_spec=grid_spec,
        compiler_params=pltpu.CompilerParams(
            dimension_semantics=("arbitrary", "arbitrary"),
            vmem_limit_bytes=VMEM_LIMIT),
        name="l0_diff_attn",
    )(qtab, ctab, zero, q, k, vt, sgb, lq1, lk1, lq2, lk2, subln_col)


def _l1_kernel(zero_ref, ya_ref, yb_ref, x0_ref, w0_ref, pg0_ref,
               g_ref, win_ref, lng_ref, lnb_ref, ws_ref, bs_ref, wout_ref, pg_ref,
               o_ref, x1_ref, u_ref, v_ref, gt_ref, vn_ref, y_ref, *raw_refs):
    tm = x0_ref.shape[0]
    h_slabs = []
    for r in range(L1_HEAD_SLABS):
        rows = slice(r * tm // L1_HEAD_SLABS, (r + 1) * tm // L1_HEAD_SLABS)
        out0 = (_dot(ya_ref[rows, :], w0_ref[0:POOL_WIDTH, :])
                + _dot(yb_ref[rows, :], w0_ref[POOL_WIDTH:, :]))
        x = x0_ref[rows, :] + _rms_norm(out0, pg0_ref[...], EPS)
        x1_ref[rows, :] = x
        h_slabs.append(_rms_norm(x, g_ref[...], EPS).astype(jnp.bfloat16))
    h = jnp.concatenate(h_slabs, axis=0)
    staged_rows = pl.ds(pl.multiple_of(zero_ref[0], tm), tm)
    nblk = D_INNER // L1_COL_BLOCK
    dests = ([(v_ref, _gelu_tanh, D_INNER)] * nblk + [(u_ref, _gelu_tanh, 0)] * nblk
             + [(gt_ref, _silu, 2 * D_INNER)] * nblk)

    def activate(i):
        dest_ref, act, _ = dests[i]
        j = i % nblk
        raw = raw_refs[i % L1_RAW_BUFS][staged_rows, :]
        dest_ref[:, j * L1_COL_BLOCK:(j + 1) * L1_COL_BLOCK] = act(raw)

    def layer_norm_v():
        vf = v_ref[...]
        mu = jnp.mean(vf, axis=-1, keepdims=True)
        vc = vf - mu
        var = jnp.mean(vc * vc, axis=-1, keepdims=True)
        vn_ref[...] = (vc * lax.rsqrt(var + EPS) * lng_ref[...] + lnb_ref[...]).astype(vn_ref.dtype)

    for i, (_, _, base) in enumerate(dests):
        col = base + (i % nblk) * L1_COL_BLOCK
        raw_refs[i % L1_RAW_BUFS][...] = _dot(h, win_ref[:, col:col + L1_COL_BLOCK])
        if i > 0:
            activate(i - 1)
        if i == nblk:
            layer_norm_v()
    activate(len(dests) - 1)

    tri_r = lax.broadcasted_iota(jnp.int32, (SGU_CHUNK, SGU_CHUNK), 0)
    tri_c = lax.broadcasted_iota(jnp.int32, (SGU_CHUNK, SGU_CHUNK), 1)
    tril = tri_c <= tri_r
    for g in range(SGU_GROUPS):
        cols = slice(g * SGU_GROUP_DIM, (g + 1) * SGU_GROUP_DIM)
        wg = jnp.where(tril, ws_ref[g], 0.0).astype(jnp.bfloat16)
        bias = bs_ref[:, g:g + 1]
        for c in range(tm // SGU_CHUNK):
            rows = slice(c * SGU_CHUNK, (c + 1) * SGU_CHUNK)
            mixed = _dot(wg, vn_ref[rows, cols]) + bias
            y_ref[rows, cols] = (u_ref[rows, cols] * mixed * gt_ref[rows, cols]).astype(y_ref.dtype)

    out = _dot(y_ref[...], wout_ref[...])
    o_ref[...] = x1_ref[...] + _rms_norm(out, pg_ref[...], EPS)


def _l1_call(ya, yb, x, w_out_prev, post_g_prev, g, w_in, ln_g, ln_b, w_s, b_s_t, w_out, post_g,
             layer):
    n = x.shape[0]
    tm = TM_L1
    tok = lambda i: (i, 0)
    const2 = lambda i: (0, 0)
    return pl.pallas_call(
        _l1_kernel,
        out_shape=jax.ShapeDtypeStruct((n, D_MODEL), jnp.float32),
        grid=(n // tm,),
        in_specs=[
            pl.BlockSpec(memory_space=pltpu.SMEM),
            pl.BlockSpec((tm, POOL_WIDTH), tok),
            pl.BlockSpec((tm, ATTN_WIDTH), tok),
            pl.BlockSpec((tm, D_MODEL), tok),
            pl.BlockSpec((None, D_INNER, D_MODEL), lambda i: (layer - 1, 0, 0),
                         pipeline_mode=pl.Buffered(1)),
            pl.BlockSpec((1, D_MODEL), const2),
            pl.BlockSpec((1, D_MODEL), const2),
            pl.BlockSpec((None, D_MODEL, IN_COLS), lambda i: (layer, 0, 0),
                         pipeline_mode=pl.Buffered(1)),
            pl.BlockSpec((1, D_INNER), const2),
            pl.BlockSpec((1, D_INNER), const2),
            pl.BlockSpec((SGU_GROUPS, SGU_CHUNK, SGU_CHUNK), lambda i: (0, 0, 0)),
            pl.BlockSpec((SGU_CHUNK, SGU_GROUPS), const2),
            pl.BlockSpec((None, D_INNER, D_MODEL), lambda i: (layer, 0, 0),
                         pipeline_mode=pl.Buffered(1)),
            pl.BlockSpec((1, D_MODEL), const2),
        ],
        out_specs=pl.BlockSpec((tm, D_MODEL), tok),
        scratch_shapes=[
            pltpu.VMEM((tm, D_MODEL), jnp.float32),
            pltpu.VMEM((tm, D_INNER), jnp.float32),
            pltpu.VMEM((tm, D_INNER), jnp.float32),
            pltpu.VMEM((tm, D_INNER), jnp.float32),
            pltpu.VMEM((tm, D_INNER), jnp.bfloat16),
            pltpu.VMEM((tm, D_INNER), jnp.bfloat16),
        ] + [pltpu.VMEM((tm, L1_COL_BLOCK), jnp.float32)] * L1_RAW_BUFS,
        compiler_params=pltpu.CompilerParams(
            dimension_semantics=("arbitrary",),
            vmem_limit_bytes=VMEM_LIMIT),
        name="l1_sgu_layer",
    )(jnp.zeros((1,), jnp.int32), ya, yb, x, w_out_prev, post_g_prev, g, w_in, ln_g, ln_b, w_s,
      b_s_t, w_out, post_g)


def _rope_lane_tables():
    half = ROPE_DIM // 2
    d = np.arange(LANES) % DIFF_QKDIM
    freq_idx = jnp.asarray(d % half, jnp.float32)
    inv_freq = jnp.power(ROPE_THETA, -freq_idx * 2.0 / ROPE_DIM)
    invf = jnp.where(jnp.asarray(d < ROPE_DIM), inv_freq, 0.0).astype(jnp.float32)
    m1 = jnp.asarray(np.where(d < half, -1.0, 0.0), jnp.float32)
    m2 = jnp.asarray(np.where((d >= half) & (d < ROPE_DIM), 1.0, 0.0), jnp.float32)
    return invf[None, :], m1[None, :], m2[None, :]


def kernel(x, positions, pre_norm, post_norm, w_in, w_out, pool_w, pool_scale, lam_q1, lam_k1, lam_q2, lam_k2, diff_subln, sgu_ln_g, sgu_ln_b, sgu_w, sgu_b):
    B, S, D = x.shape
    bf16 = jnp.bfloat16
    w_in_b = w_in.astype(bf16)
    w_out_b = w_out.astype(bf16)
    invf, m1, m2 = _rope_lane_tables()
    pos_f = positions.astype(jnp.float32).reshape(B, S // TM_IN0, 1, TM_IN0)

    q, k, vt, ya, sgb = _in0_call(x, pos_f, pre_norm[0:1], w_in_b, pool_w[0].astype(bf16),
                                  pool_scale[0:1], invf, m1, m2, layer=0)
    lam_init = 0.8 - 0.6 * math.exp(-0.3 * 0)
    yb = _attn_call(q, k, vt, sgb, lam_q1[0:1], lam_k1[0:1], lam_q2[0:1], lam_k2[0:1],
                    diff_subln[0].reshape(DIFF_VDIM, 1), lam_init)
    n = B * S

    x2 = _l1_call(ya.reshape(n, POOL_WIDTH), yb.reshape(n, ATTN_WIDTH), x.reshape(n, D),
                  w_out_b, post_norm[0:1], pre_norm[1:2], w_in_b, sgu_ln_g[0:1], sgu_ln_b[0:1],
                  sgu_w[0], sgu_b[0].T, w_out_b, post_norm[1:2], layer=1)
    return x2.reshape(B, S, D)
```
